```python
import math
import jax, jax.numpy as jnp
from jax import lax
import numpy as np

D_MODEL = 4096
BATCH = 1
SEQ = 8192
DEPTH = 1
DEC_BATCH = 16
DEC_SEQ = 64
PAST_LEN = 2048

CHUNK = 64
Q_BLOCK = 128
DIFF_HEADS = 16
DIFF_DK = 64
DIFF_DV = 2 * DIFF_DK
SB_HEADS = 16
SB_DH = 128
ROPE_THETA = 500000.0
ROPE_DIMS = DIFF_DK // 4
N_EXPERTS = 128
TOP_K = 8
EXPERT_HIDDEN = 512
SHARED_HIDDEN = 1024
ROUTE_SCALE = 2.5
MOE_BLOCK = 128
LN_EPS = 1e-5
DEEPNORM_ALPHA = (2.0 * DEPTH) ** 0.25
DEEPNORM_BETA = (8.0 * DEPTH) ** -0.25
DIFF_QK_W = DIFF_HEADS * 2 * DIFF_DK
DIFF_W = DIFF_HEADS * DIFF_DV
SB_W = SB_HEADS * SB_DH
IN_COLS = 2 * DIFF_QK_W + DIFF_W + 3 * SB_W + 2 * D_MODEL

kernel_name = 'diff_stickbreak_moe_stream_step'


def _layer_norm(x):
    xf = x.astype(jnp.float32)
    xc = xf - jnp.mean(xf, axis=-1, keepdims=True)
    return xc * lax.rsqrt(jnp.mean(xc * xc, axis=-1, keepdims=True) + LN_EPS)


def _post_norm(x, g, b):
    return (_layer_norm(x) * g.astype(jnp.float32) + b.astype(jnp.float32)).astype(x.dtype)


def _modulate(x, shift, scale):
    y = _layer_norm(x) * (1.0 + scale[:, None, :].astype(jnp.float32)) + shift[:, None, :].astype(jnp.float32)
    return y.astype(x.dtype)


def _partial_rope(x, pos):
    half = ROPE_DIMS // 2
    inv_freq = ROPE_THETA ** (-jnp.arange(half, dtype=jnp.float32) / half)
    ang = pos.astype(jnp.float32)[:, None] * inv_freq[None, :]
    shape = (1, pos.shape[0]) + (1,) * (x.ndim - 3) + (half,)
    cos = jnp.cos(ang).reshape(shape)
    sin = jnp.sin(ang).reshape(shape)
    xf = x.astype(jnp.float32)
    x1 = xf[..., :half]
    x2 = xf[..., half:ROPE_DIMS]
    out = jnp.concatenate([x1 * cos - x2 * sin, x1 * sin + x2 * cos, xf[..., ROPE_DIMS:]], axis=-1)
    return out.astype(x.dtype)


def _sweep_query_blocks(attn_block, q, q_pos):
    b, t = q.shape[:2]
    if t <= Q_BLOCK or t % Q_BLOCK:
        return attn_block(q, q_pos)
    nb = t // Q_BLOCK
    qb = jnp.moveaxis(q.reshape((b, nb, Q_BLOCK) + q.shape[2:]), 1, 0)
    ob = lax.map(lambda a: attn_block(a[0], a[1]), (qb, q_pos.reshape(nb, Q_BLOCK)))
    return jnp.moveaxis(ob, 0, 1).reshape((b, t) + ob.shape[3:])


def _diff_attn_block(q, q_pos, k, v, k_pos, lam):
    s = jnp.einsum('bqhcd,bkhcd->bhcqk', q, k).astype(jnp.float32) * (DIFF_DK ** -0.5)
    mask = (k_pos[None, :] // CHUNK) <= (q_pos[:, None] // CHUNK)
    p = jax.nn.softmax(jnp.where(mask, s, -1e30), axis=-1)
    a = p[:, :, 0] - lam * p[:, :, 1]
    return jnp.einsum('bhqk,bkhe->bqhe', a.astype(v.dtype), v)


def _stick_breaking_block(q, q_pos, k, v, k_pos):
    z = jnp.einsum('bqhd,bkhd->bhqk', q, k).astype(jnp.float32) * (SB_DH ** -0.5)
    mask = k_pos[None, :] < q_pos[:, None]
    log_keep = jnp.where(mask, jax.nn.log_sigmoid(-z), 0.0)
    later = lax.cumsum(log_keep, axis=3, reverse=True) - log_keep
    w = jnp.where(mask, jnp.exp(jax.nn.log_sigmoid(z) + later), 0.0)
    return jnp.einsum('bhqk,bkhd->bqhd', w.astype(v.dtype), v)


def _token_mixers(h, past, w_in, w_br_diff, w_br_sb, w_out, lam_q1, lam_k1, lam_q2, lam_k2, diff_norm_g, lam_init):
    b, t, _ = h.shape
    past_len = 0 if past is None else past[0].shape[1]
    q_pos = past_len + jnp.arange(t, dtype=jnp.int32)
    k_pos = jnp.arange(past_len + t, dtype=jnp.int32)
    proj = h @ w_in
    widths = [DIFF_QK_W, DIFF_QK_W, DIFF_W, SB_W, SB_W, SB_W, D_MODEL, D_MODEL]
    offsets = np.cumsum(widths)[:-1].tolist()
    qd, kd, vd, qs, ks, vs, gd, gs = jnp.split(proj, offsets, axis=-1)
    qd = _partial_rope(qd.reshape(b, t, DIFF_HEADS, 2, DIFF_DK), q_pos)
    kd = _partial_rope(kd.reshape(b, t, DIFF_HEADS, 2, DIFF_DK), q_pos)
    vd = vd.reshape(b, t, DIFF_HEADS, DIFF_DV)
    qs = qs.reshape(b, t, SB_HEADS, SB_DH)
    ks = ks.reshape(b, t, SB_HEADS, SB_DH)
    vs = vs.reshape(b, t, SB_HEADS, SB_DH)
    if past is None:
        kd_all, vd_all, ks_all, vs_all = kd, vd, ks, vs
    else:
        kd_all = jnp.concatenate([past[0], kd], axis=1)
        vd_all = jnp.concatenate([past[1], vd], axis=1)
        ks_all = jnp.concatenate([past[2], ks], axis=1)
        vs_all = jnp.concatenate([past[3], vs], axis=1)
    lam = (jnp.exp(jnp.sum(lam_q1.astype(jnp.float32) * lam_k1.astype(jnp.float32)))
           - jnp.exp(jnp.sum(lam_q2.astype(jnp.float32) * lam_k2.astype(jnp.float32))) + lam_init)
    od = _sweep_query_blocks(lambda q, p: _diff_attn_block(q, p, kd_all, vd_all, k_pos, lam), qd, q_pos)
    osb = _sweep_query_blocks(lambda q, p: _stick_breaking_block(q, p, ks_all, vs_all, k_pos), qs, q_pos)
    odf = od.astype(jnp.float32)
    odf = odf * lax.rsqrt(jnp.mean(odf * odf, axis=-1, keepdims=True) + LN_EPS)
    od = (odf * diff_norm_g.astype(jnp.float32) * (1.0 - lam_init)).astype(h.dtype)
    branch_d = od.reshape(b, t, DIFF_W) @ w_br_diff
    branch_s = osb.reshape(b, t, SB_W) @ w_br_sb
    merged = jax.nn.sigmoid(gd) * branch_d + jax.nn.sigmoid(gs) * branch_s
    return merged @ w_out, (kd, vd, ks, vs)


def _swiglu(x, wg, wu, wd):
    return (jax.nn.silu(x @ wg) * (x @ wu)) @ wd


def _moe(h, w_router, router_bias, w_e_gate, w_e_up, w_e_down, w_sh_gate, w_sh_up, w_sh_down):
    t, d = h.shape
    scores = jax.nn.sigmoid((h @ w_router).astype(jnp.float32))
    _, idx = lax.top_k(scores + router_bias.astype(jnp.float32), TOP_K)
    gate = jnp.take_along_axis(scores, idx, axis=-1)
    gate = gate / jnp.sum(gate, axis=-1, keepdims=True) * ROUTE_SCALE
    tk = t * TOP_K
    flat_e = idx.reshape(tk)
    order = jnp.argsort(flat_e)
    sorted_e = flat_e[order]
    sorted_tok = (order // TOP_K).astype(jnp.int32)
    sorted_gate = gate.reshape(tk)[order]
    counts = jnp.zeros((N_EXPERTS,), jnp.int32).at[flat_e].add(1)
    starts = jnp.cumsum(counts) - counts
    padded = (counts + MOE_BLOCK - 1) // MOE_BLOCK * MOE_BLOCK
    pad_ends = jnp.cumsum(padded)
    pad_starts = pad_ends - padded
    dest = pad_starts[sorted_e] + jnp.arange(tk, dtype=jnp.int32) - starts[sorted_e]
    n_blocks = -(-tk // MOE_BLOCK) + N_EXPERTS
    n_slots = n_blocks * MOE_BLOCK
    slot_tok = jnp.full((n_slots,), t, jnp.int32).at[dest].set(sorted_tok)
    slot_gate = jnp.zeros((n_slots,), jnp.float32).at[dest].set(sorted_gate)
    block_pos = jnp.arange(n_blocks, dtype=jnp.int32) * MOE_BLOCK
    block_expert = jnp.minimum(jnp.searchsorted(pad_ends, block_pos, side='right'), N_EXPERTS - 1)
    h_pad = jnp.concatenate([h, jnp.zeros((1, d), h.dtype)], axis=0)

    def expert_block(acc, blk):
        tok, g, e = blk
        yb = _swiglu(h_pad[tok], w_e_gate[e], w_e_up[e], w_e_down[e])
        return acc.at[tok].add(yb.astype(jnp.float32) * g[:, None]), None

    acc, _ = lax.scan(expert_block, jnp.zeros((t + 1, d), jnp.float32),
                      (slot_tok.reshape(n_blocks, MOE_BLOCK), slot_gate.reshape(n_blocks, MOE_BLOCK), block_expert))
    shared = _swiglu(h, w_sh_gate, w_sh_up, w_sh_down).astype(jnp.float32)
    return (acc[:t] + shared).astype(h.dtype)


def _layer(x, c, past, lam_init, p):
    (w_ada, b_ada, w_in, w_br_diff, w_br_sb, w_out, lam_q1, lam_k1, lam_q2, lam_k2, diff_norm_g,
     ln1_g, ln1_b, w_router, router_bias, w_e_gate, w_e_up, w_e_down, w_sh_gate, w_sh_up, w_sh_down,
     ln2_g, ln2_b) = p
    b, t, d = x.shape
    mod = (jax.nn.silu(c) @ w_ada + b_ada).reshape(b, 6, d)
    shift_a, scale_a, gate_a, shift_f, scale_f, gate_f = (mod[:, i] for i in range(6))
    h = _modulate(x, shift_a, scale_a)
    a, rows = _token_mixers(h, past, w_in, w_br_diff, w_br_sb, w_out, lam_q1, lam_k1, lam_q2, lam_k2,
                            diff_norm_g, lam_init)
    x = _post_norm(DEEPNORM_ALPHA * x + gate_a[:, None, :] * a, ln1_g, ln1_b)
    h = _modulate(x, shift_f, scale_f)
    f = _moe(h.reshape(b * t, d), w_router, router_bias, w_e_gate, w_e_up, w_e_down,
             w_sh_gate, w_sh_up, w_sh_down).reshape(b, t, d)
    x = _post_norm(DEEPNORM_ALPHA * x + gate_f[:, None, :] * f, ln2_g, ln2_b)
    return x, rows


def setup_inputs(seed: int = 0) -> dict:
    key = jax.random.key(seed)
    ks = jax.random.split(key, 40)
    f32 = jnp.float32
    L, D = DEPTH, D_MODEL

    def nrm(k, shape, scale):
        return jax.random.normal(k, shape, f32) * scale

    s_in = D ** -0.5
    beta = DEEPNORM_BETA
    w_in = jnp.concatenate([
        nrm(ks[10], (L, D, DIFF_QK_W), s_in),
        nrm(ks[11], (L, D, DIFF_QK_W), s_in),
        nrm(ks[12], (L, D, DIFF_W), s_in * beta),
        nrm(ks[13], (L, D, SB_W), s_in),
        nrm(ks[14], (L, D, SB_W), s_in),
        nrm(ks[15], (L, D, SB_W), s_in * beta),
        nrm(ks[16], (L, D, 2 * D), s_in)], axis=-1)
    return {
        'x_prompt': nrm(ks[0], (BATCH, SEQ, D), 1.0),
        'x_sample': nrm(ks[1], (DEC_BATCH, DEC_SEQ, D), 1.0),
        'c_prompt': nrm(ks[2], (BATCH, D), 1.0),
        'c_sample': nrm(ks[3], (DEC_BATCH, D), 1.0),
        'cache_diff_k': nrm(ks[4], (L, DEC_BATCH, PAST_LEN, DIFF_HEADS, 2, DIFF_DK), 1.0),
        'cache_diff_v': nrm(ks[5], (L, DEC_BATCH, PAST_LEN, DIFF_HEADS, DIFF_DV), beta),
        'cache_sb_k': nrm(ks[6], (L, DEC_BATCH, PAST_LEN, SB_HEADS, SB_DH), 1.0),
        'cache_sb_v': nrm(ks[7], (L, DEC_BATCH, PAST_LEN, SB_HEADS, SB_DH), beta),
        'w_ada': nrm(ks[8], (L, D, 6 * D), 0.5 * s_in),
        'b_ada': nrm(ks[9], (L, 6 * D), 0.02),
        'w_in': w_in,
        'w_br_diff': nrm(ks[17], (L, DIFF_W, D), DIFF_W ** -0.5 * beta),
        'w_br_sb': nrm(ks[18], (L, SB_W, D), SB_W ** -0.5 * beta),
        'w_out': nrm(ks[19], (L, D, D), s_in * beta),
        'lam_q1': nrm(ks[20], (L, DIFF_DK), 0.1),
        'lam_k1': nrm(ks[21], (L, DIFF_DK), 0.1),
        'lam_q2': nrm(ks[22], (L, DIFF_DK), 0.1),
        'lam_k2': nrm(ks[23], (L, DIFF_DK), 0.1),
        'diff_norm_g': 1.0 + nrm(ks[24], (L, DIFF_DV), 0.02),
        'ln1_g': 1.0 + nrm(ks[25], (L, D), 0.02),
        'ln1_b': nrm(ks[26], (L, D), 0.02),
        'w_router': nrm(ks[27], (L, D, N_EXPERTS), s_in),
        'router_bias': nrm(ks[28], (L, N_EXPERTS), 0.01),
        'w_e_gate': nrm(ks[29], (L, N_EXPERTS, D, EXPERT_HIDDEN), s_in),
        'w_e_up': nrm(ks[30], (L, N_EXPERTS, D, EXPERT_HIDDEN), s_in),
        'w_e_down': nrm(ks[31], (L, N_EXPERTS, EXPERT_HIDDEN, D), EXPERT_HIDDEN ** -0.5 * beta),
        'w_sh_gate': nrm(ks[32], (L, D, SHARED_HIDDEN), s_in),
        'w_sh_up': nrm(ks[33], (L, D, SHARED_HIDDEN), s_in),
        'w_sh_down': nrm(ks[34], (L, SHARED_HIDDEN, D), SHARED_HIDDEN ** -0.5 * beta),
        'ln2_g': 1.0 + nrm(ks[35], (L, D), 0.02),
        'ln2_b': nrm(ks[36], (L, D), 0.02),
    }


def reference(x_prompt, x_sample, c_prompt, c_sample, cache_diff_k, cache_diff_v, cache_sb_k, cache_sb_v,
              w_ada, b_ada, w_in, w_br_diff, w_br_sb, w_out, lam_q1, lam_k1, lam_q2, lam_k2, diff_norm_g,
              ln1_g, ln1_b, w_router, router_bias, w_e_gate, w_e_up, w_e_down, w_sh_gate, w_sh_up, w_sh_down,
              ln2_g, ln2_b):
    xp, xs = x_prompt, x_sample
    rows_p, rows_s = [], []
    for l in range(DEPTH):
        lam_init = 0.8 - 0.6 * math.exp(-0.3 * l)
        p = (w_ada[l], b_ada[l], w_in[l], w_br_diff[l], w_br_sb[l], w_out[l], lam_q1[l], lam_k1[l],
             lam_q2[l], lam_k2[l], diff_norm_g[l], ln1_g[l], ln1_b[l], w_router[l], router_bias[l],
             w_e_gate[l], w_e_up[l], w_e_down[l], w_sh_gate[l], w_sh_up[l], w_sh_down[l], ln2_g[l], ln2_b[l])
        xp, r_p = _layer(xp, c_prompt, None, lam_init, p)
        xs, r_s = _layer(xs, c_sample, (cache_diff_k[l], cache_diff_v[l], cache_sb_k[l], cache_sb_v[l]), lam_init, p)
        rows_p.append(r_p)
        rows_s.append(r_s)
    new_diff_k_prompt = jnp.stack([r[0] for r in rows_p])
    new_diff_v_prompt = jnp.stack([r[1] for r in rows_p])
    new_sb_k_prompt = jnp.stack([r[2] for r in rows_p])
    new_sb_v_prompt = jnp.stack([r[3] for r in rows_p])
    new_diff_k_sample = jnp.stack([r[0] for r in rows_s])
    new_diff_v_sample = jnp.stack([r[1] for r in rows_s])
    new_sb_k_sample = jnp.stack([r[2] for r in rows_s])
    new_sb_v_sample = jnp.stack([r[3] for r in rows_s])
    return (xp, xs, new_diff_k_prompt, new_diff_v_prompt, new_sb_k_prompt, new_sb_v_prompt,
            new_diff_k_sample, new_diff_v_sample, new_sb_k_sample, new_sb_v_sample)
```

```python
import functools

import jax
import jax.numpy as jnp
import numpy as np
from jax import lax
from jax.experimental import pallas as pl
from jax.experimental.pallas import tpu as pltpu

F32 = jnp.float32
BF16 = jnp.bfloat16

CHUNK = 64
DIFF_DK = 64
HEAD_W = 128
ROPE_THETA = 500000.0
ROPE_DIMS = DIFF_DK // 4
TOP_K = 8
ROUTE_SCALE = 2.5
LN_EPS = 1e-5
LANES = 128
VMEM_LIMIT = 52 * 1024 * 1024
SB_DEAD_LOG = -104.0


def _cparams(sem):
    return pltpu.CompilerParams(dimension_semantics=sem, vmem_limit_bytes=VMEM_LIMIT)


def _tile(n, pref):
    if n <= pref:
        return n
    t = pref - pref % 64
    while n % t:
        t -= 64
    assert t > 0, (n, pref)
    return t


def _ln(x):
    xc = x - jnp.mean(x, axis=-1, keepdims=True)
    return xc * lax.rsqrt(jnp.mean(xc * xc, axis=-1, keepdims=True) + LN_EPS)


def _dot(a, b):
    return jnp.dot(a, b, preferred_element_type=F32)


def _dot_t(a, b):
    return lax.dot_general(a, b, (((1,), (1,)), ((), ())), preferred_element_type=F32)


def _ada_kernel(c_ref, w_ref, b_ref, o_ref):
    c = c_ref[...]
    s = (c * jax.nn.sigmoid(c)).astype(BF16)
    o_ref[...] = _dot(s, w_ref[...].astype(BF16)) + b_ref[...]


def _ada(c, w_ada, b_ada):
    m, d = c.shape
    n = w_ada.shape[1]
    tn = _tile(n, 512)
    return pl.pallas_call(
        _ada_kernel,
        grid=(n // tn,),
        in_specs=[pl.BlockSpec((m, d), lambda j: (0, 0)),
                  pl.BlockSpec((d, tn), lambda j: (0, j)),
                  pl.BlockSpec((1, tn), lambda j: (0, j))],
        out_specs=pl.BlockSpec((m, tn), lambda j: (0, j)),
        out_shape=jax.ShapeDtypeStruct((m, n), F32),
        compiler_params=_cparams(("arbitrary",)),
        name="ada",
    )(c, w_ada, b_ada.reshape(1, n))


def _ln_mod_kernel(x_ref, shift_ref, scale_ref, h_ref):
    h_ref[...] = (_ln(x_ref[...]) * (1.0 + scale_ref[...]) + shift_ref[...]).astype(h_ref.dtype)


def _mod_spec(d, comp, tiles_per_seq, row0=0):
    return pl.BlockSpec((None, 1, d), lambda i: ((row0 + i // tiles_per_seq) * 6 + comp, 0, 0))


def _ln_mod(x, mod3, row0, seq_len, shift_comp, scale_comp):
    t, d = x.shape
    tm = _tile(seq_len, 256)
    tps = seq_len // tm
    return pl.pallas_call(
        _ln_mod_kernel,
        grid=(t // tm,),
        in_specs=[pl.BlockSpec((tm, d), lambda i: (i, 0)),
                  _mod_spec(d, shift_comp, tps, row0),
                  _mod_spec(d, scale_comp, tps, row0)],
        out_specs=pl.BlockSpec((tm, d), lambda i: (i, 0)),
        out_shape=jax.ShapeDtypeStruct((t, d), BF16),
        compiler_params=_cparams(("parallel",)),
        name="ln_mod",
    )(x, mod3, mod3)


def _proj_kernel(*refs, rope, scale, n_out):
    x_ref, w_ref = refs[0], refs[1]
    outs = refs[len(refs) - n_out:]
    acc = _dot(x_ref[...], w_ref[...])
    if rope:
        c_ref, s1_ref, s2_ref = refs[2:5]
        cos, s1, s2 = c_ref[...], s1_ref[...], s2_ref[...]
        parts = []
        for g in range(acc.shape[1] // LANES):
            blk = acc[:, g * LANES:(g + 1) * LANES]
            half = ROPE_DIMS // 2
            parts.append(blk * cos + pltpu.roll(blk, LANES - half, 1) * s1 + pltpu.roll(blk, half, 1) * s2)
        acc = jnp.concatenate(parts, axis=1) if len(parts) > 1 else parts[0]
    if scale != 1.0:
        acc = acc * scale
    for o in outs:
        o[...] = acc.astype(o.dtype)


def _proj(x, w, out_dtypes, rope_tabs=None, scale=1.0, tm_pref=512, tn_pref=1024):
    m, k = x.shape
    n = w.shape[1]
    tm, tn = _tile(m, tm_pref), _tile(n, tn_pref)
    in_specs = [pl.BlockSpec((tm, k), lambda i, j: (i, 0)),
                pl.BlockSpec((k, tn), lambda i, j: (0, j))]
    args = [x, w]
    if rope_tabs is not None:
        in_specs += [pl.BlockSpec((tm, LANES), lambda i, j: (i, 0))] * 3
        args += list(rope_tabs)
    outs = pl.pallas_call(
        functools.partial(_proj_kernel, rope=rope_tabs is not None, scale=scale, n_out=len(out_dtypes)),
        grid=(m // tm, n // tn),
        in_specs=in_specs,
        out_specs=[pl.BlockSpec((tm, tn), lambda i, j: (i, j)) for _ in out_dtypes],
        out_shape=[jax.ShapeDtypeStruct((m, n), dt) for dt in out_dtypes],
        compiler_params=_cparams(("parallel", "arbitrary")),
        name="proj",
    )(*args)
    return outs


def _glu_kernel(x_ref, wg_ref, wu_ref, o_ref):
    x = x_ref[...]
    g = _dot(x, wg_ref[...])
    u = _dot(x, wu_ref[...])
    o_ref[...] = (g * jax.nn.sigmoid(g) * u).astype(o_ref.dtype)


def _glu(x, wg, wu):
    m, k = x.shape
    n = wg.shape[1]
    tm, tn = _tile(m, 512), _tile(n, 512)
    return pl.pallas_call(
        _glu_kernel,
        grid=(m // tm, n // tn),
        in_specs=[pl.BlockSpec((tm, k), lambda i, j: (i, 0)),
                  pl.BlockSpec((k, tn), lambda i, j: (0, j)),
                  pl.BlockSpec((k, tn), lambda i, j: (0, j))],
        out_specs=pl.BlockSpec((tm, tn), lambda i, j: (i, j)),
        out_shape=jax.ShapeDtypeStruct((m, n), BF16),
        compiler_params=_cparams(("parallel", "arbitrary")),
        name="shared_glu",
    )(x, wg, wu)


def _merge_kernel(od_ref, os_ref, wd_ref, ws_ref, gd_ref, gs_ref, o_ref):
    bd = _dot(od_ref[...], wd_ref[...])
    bs = _dot(os_ref[...], ws_ref[...])
    gd = jax.nn.sigmoid(gd_ref[...].astype(F32))
    gs = jax.nn.sigmoid(gs_ref[...].astype(F32))
    o_ref[...] = (gd * bd + gs * bs).astype(o_ref.dtype)


def _merge(od, osb, wbd, wbs, gd, gs):
    m, k = od.shape
    n = wbd.shape[1]
    tm, tn = _tile(m, 512), _tile(n, 1024)
    return pl.pallas_call(
        _merge_kernel,
        grid=(m // tm, n // tn),
        in_specs=[pl.BlockSpec((tm, k), lambda i, j: (i, 0)),
                  pl.BlockSpec((tm, k), lambda i, j: (i, 0)),
                  pl.BlockSpec((k, tn), lambda i, j: (0, j)),
                  pl.BlockSpec((k, tn), lambda i, j: (0, j)),
                  pl.BlockSpec((tm, tn), lambda i, j: (i, j)),
                  pl.BlockSpec((tm, tn), lambda i, j: (i, j))],
        out_specs=pl.BlockSpec((tm, tn), lambda i, j: (i, j)),
        out_shape=jax.ShapeDtypeStruct((m, n), BF16),
        compiler_params=_cparams(("parallel", "arbitrary")),
        name="merge",
    )(od, osb, wbd, wbs, gd, gs)


def _lambda(lam_ref, lam_init):
    a = jnp.sum(lam_ref[0:1, :] * lam_ref[1:2, :], axis=-1, keepdims=True)
    b = jnp.sum(lam_ref[2:3, :] * lam_ref[3:4, :], axis=-1, keepdims=True)
    return jnp.exp(a) - jnp.exp(b) + lam_init


def _split_components(q):
    lane = lax.broadcasted_iota(jnp.int32, q.shape, 1)
    zero = jnp.zeros_like(q)
    return jnp.where(lane < DIFF_DK, q, zero), jnp.where(lane >= DIFF_DK, q, zero)


def _diff_finish(o1, l1, o2, l2, lam, g, lam_init):
    o = o1 / l1 - lam * (o2 / l2)
    o = o * lax.rsqrt(jnp.mean(o * o, axis=-1, keepdims=True) + LN_EPS)
    return o * g * (1.0 - lam_init)


def _diff_prompt_kernel(lam_ref, g_ref, q_ref, k_ref, v_ref, o_ref, m_ref, l_ref, acc_ref, *, tq, lam_init):
    i = pl.program_id(1)
    q1, q2 = _split_components(q_ref[...])
    m_ref[...] = jnp.full(m_ref.shape, -jnp.inf, F32)
    l_ref[...] = jnp.zeros(l_ref.shape, F32)
    acc_ref[...] = jnp.zeros(acc_ref.shape, F32)

    def block(j, masked):
        k = k_ref[pl.ds(pl.multiple_of(j * tq, tq), tq), :]
        v = v_ref[pl.ds(pl.multiple_of(j * tq, tq), tq), :]
        if masked:
            r = lax.broadcasted_iota(jnp.int32, (tq, tq), 0) // CHUNK
            c = lax.broadcasted_iota(jnp.int32, (tq, tq), 1) // CHUNK
            keep = c <= r
        for comp, qc in enumerate((q1, q2)):
            s = _dot_t(qc, k)
            if masked:
                s = jnp.where(keep, s, -1e30)
            m_old = m_ref[comp]
            m_new = jnp.maximum(m_old, jnp.max(s, axis=-1, keepdims=True))
            alpha = jnp.exp(m_old - m_new)
            p = jnp.exp(s - m_new)
            l_ref[comp] = alpha * l_ref[comp] + jnp.sum(p, axis=-1, keepdims=True)
            acc_ref[comp] = alpha * acc_ref[comp] + _dot(p.astype(BF16), v)
            m_ref[comp] = m_new

    def body(j, carry):
        block(j, False)
        return carry

    lax.fori_loop(0, i, body, 0)
    block(i, True)
    lam = _lambda(lam_ref, lam_init)
    o = _diff_finish(acc_ref[0], l_ref[0], acc_ref[1], l_ref[1], lam, g_ref[...], lam_init)
    o_ref[...] = o.astype(o_ref.dtype)


def _diff_prompt(q, k, v, lam4, g, lam_init):
    t, w = q.shape
    heads = w // HEAD_W
    tq = _tile(t, 256)
    assert tq % CHUNK == 0
    return pl.pallas_call(
        functools.partial(_diff_prompt_kernel, tq=tq, lam_init=lam_init),
        grid=(heads, t // tq),
        in_specs=[pl.BlockSpec((4, DIFF_DK), lambda h, i: (0, 0)),
                  pl.BlockSpec((1, HEAD_W), lambda h, i: (0, 0)),
                  pl.BlockSpec((tq, HEAD_W), lambda h, i: (i, h)),
                  pl.BlockSpec((t, HEAD_W), lambda h, i: (0, h)),
                  pl.BlockSpec((t, HEAD_W), lambda h, i: (0, h))],
        out_specs=pl.BlockSpec((tq, HEAD_W), lambda h, i: (i, h)),
        out_shape=jax.ShapeDtypeStruct((t, w), BF16),
        scratch_shapes=[pltpu.VMEM((2, tq, 1), F32), pltpu.VMEM((2, tq, 1), F32),
                        pltpu.VMEM((2, tq, HEAD_W), F32)],
        compiler_params=_cparams(("parallel", "arbitrary")),
        name="diff_prompt",
    )(lam4, g, q, k, v)


def _diff_sample_kernel(lam_ref, g_ref, q_ref, kn_ref, vn_ref, kc_ref, vc_ref, o_ref, *, past_len, lam_init):
    q1, q2 = _split_components(q_ref[...])
    kp = kc_ref[...].astype(BF16)
    vp = vc_ref[...].astype(BF16)
    kn = kn_ref[...].astype(BF16)
    vn = vn_ref[...].astype(BF16)
    tn = kn.shape[0]
    r = (past_len + lax.broadcasted_iota(jnp.int32, (tn, tn), 0)) // CHUNK
    c = (past_len + lax.broadcasted_iota(jnp.int32, (tn, tn), 1)) // CHUNK
    keep = c <= r
    res = []
    for qc in (q1, q2):
        sp = _dot_t(qc, kp)
        sn = jnp.where(keep, _dot_t(qc, kn), -1e30)
        m = jnp.maximum(jnp.max(sp, axis=-1, keepdims=True), jnp.max(sn, axis=-1, keepdims=True))
        pp = jnp.exp(sp - m)
        pn = jnp.exp(sn - m)
        l = jnp.sum(pp, axis=-1, keepdims=True) + jnp.sum(pn, axis=-1, keepdims=True)
        res.append((_dot(pp.astype(BF16), vp) + _dot(pn.astype(BF16), vn), l))
    lam = _lambda(lam_ref, lam_init)
    o = _diff_finish(res[0][0], res[0][1], res[1][0], res[1][1], lam, g_ref[...], lam_init)
    o_ref[...] = o.astype(o_ref.dtype)


def _diff_sample(q, kn, vn, kc, vc, lam4, g, lam_init, seq_len):
    t, w = q.shape
    heads = w // HEAD_W
    b, past_len, _ = kc.shape
    assert past_len % CHUNK == 0
    row = pl.BlockSpec((seq_len, HEAD_W), lambda bi, h: (bi, h))
    cache = pl.BlockSpec((None, past_len, HEAD_W), lambda bi, h: (bi, 0, h))
    return pl.pallas_call(
        functools.partial(_diff_sample_kernel, past_len=past_len, lam_init=lam_init),
        grid=(b, heads),
        in_specs=[pl.BlockSpec((4, DIFF_DK), lambda bi, h: (0, 0)),
                  pl.BlockSpec((1, HEAD_W), lambda bi, h: (0, 0)),
                  row, row, row, cache, cache],
        out_specs=row,
        out_shape=jax.ShapeDtypeStruct((t, w), BF16),
        compiler_params=_cparams(("parallel", "arbitrary")),
        name="diff_sample",
    )(lam4, g, q, kn, vn, kc, vc)


def _sb_block(q, k, v, carry_ref, acc_ref, keep):
    tk = k.shape[0]
    z = _dot_t(q, k)
    soft = jnp.log(1.0 + jnp.exp(-jnp.abs(z)))
    log_sig = jnp.minimum(z, 0.0) - soft
    log_keep = jnp.minimum(-z, 0.0) - soft
    if keep is not None:
        log_keep = jnp.where(keep, log_keep, 0.0)
    upper = (lax.broadcasted_iota(jnp.int32, (tk, tk), 0) > lax.broadcasted_iota(jnp.int32, (tk, tk), 1))
    upper = jnp.where(upper, 1.0, 0.0).astype(BF16)
    hi = log_keep.astype(BF16)
    lo = (log_keep - hi.astype(F32)).astype(BF16)
    later = _dot(hi, upper) + _dot(lo, upper) + carry_ref[...]
    w = jnp.exp(log_sig + later)
    if keep is not None:
        w = jnp.where(keep, w, 0.0)
    acc_ref[...] += _dot(w.astype(BF16), v)
    carry_ref[...] += jnp.sum(log_keep, axis=-1, keepdims=True)


def _sb_prompt_kernel(q_ref, k_ref, v_ref, o_ref, carry_ref, acc_ref, *, tq):
    i = pl.program_id(1)
    q = q_ref[...]
    carry_ref[...] = jnp.zeros(carry_ref.shape, F32)
    acc_ref[...] = jnp.zeros(acc_ref.shape, F32)

    def load(j):
        start = pl.multiple_of(j * tq, tq)
        return k_ref[pl.ds(start, tq), :], v_ref[pl.ds(start, tq), :]

    strict = (lax.broadcasted_iota(jnp.int32, (tq, tq), 1) < lax.broadcasted_iota(jnp.int32, (tq, tq), 0))
    k, v = load(i)
    _sb_block(q, k, v, carry_ref, acc_ref, strict)

    def cond(state):
        j, alive = state
        return jnp.logical_and(j >= 0, alive > SB_DEAD_LOG)

    def body(state):
        j, _ = state
        kj, vj = load(j)
        _sb_block(q, kj, vj, carry_ref, acc_ref, None)
        return j - 1, jnp.max(carry_ref[...])

    lax.while_loop(cond, body, (i - 1, jnp.max(carry_ref[...])))
    o_ref[...] = acc_ref[...].astype(o_ref.dtype)


def _sb_prompt(q, k, v):
    t, w = q.shape
    heads = w // HEAD_W
    tq = _tile(t, 256)
    return pl.pallas_call(
        functools.partial(_sb_prompt_kernel, tq=tq),
        grid=(heads, t // tq),
        in_specs=[pl.BlockSpec((tq, HEAD_W), lambda h, i: (i, h)),
                  pl.BlockSpec((t, HEAD_W), lambda h, i: (0, h)),
                  pl.BlockSpec((t, HEAD_W), lambda h, i: (0, h))],
        out_specs=pl.BlockSpec((tq, HEAD_W), lambda h, i: (i, h)),
        out_shape=jax.ShapeDtypeStruct((t, w), BF16),
        scratch_shapes=[pltpu.VMEM((tq, 1), F32), pltpu.VMEM((tq, HEAD_W), F32)],
        compiler_params=_cparams(("parallel", "arbitrary")),
        name="sb_prompt",
    )(q, k, v)


def _sb_sample_kernel(q_ref, kn_ref, vn_ref, kc_ref, vc_ref, o_ref, carry_ref, acc_ref, *, tk):
    q = q_ref[...]
    carry_ref[...] = jnp.zeros(carry_ref.shape, F32)
    acc_ref[...] = jnp.zeros(acc_ref.shape, F32)
    tn = kn_ref.shape[0]
    strict = (lax.broadcasted_iota(jnp.int32, (tn, tn), 1) < lax.broadcasted_iota(jnp.int32, (tn, tn), 0))
    _sb_block(q, kn_ref[...].astype(BF16), vn_ref[...].astype(BF16), carry_ref, acc_ref, strict)
    past_len = kc_ref.shape[0]
    for j in reversed(range(past_len // tk)):
        @pl.when(jnp.max(carry_ref[...]) > SB_DEAD_LOG)
        def _():
            kj = kc_ref[j * tk:(j + 1) * tk, :].astype(BF16)
            vj = vc_ref[j * tk:(j + 1) * tk, :].astype(BF16)
            _sb_block(q, kj, vj, carry_ref, acc_ref, None)
    o_ref[...] = acc_ref[...].astype(o_ref.dtype)


def _sb_sample(q, kn, vn, kc, vc, seq_len):
    t, w = q.shape
    heads = w // HEAD_W
    b, past_len, _ = kc.shape
    tk = _tile(past_len, 256)
    row = pl.BlockSpec((seq_len, HEAD_W), lambda bi, h: (bi, h))
    cache = pl.BlockSpec((None, past_len, HEAD_W), lambda bi, h: (bi, 0, h))
    return pl.pallas_call(
        functools.partial(_sb_sample_kernel, tk=tk),
        grid=(b, heads),
        in_specs=[row, row, row, cache, cache],
        out_specs=row,
        out_shape=jax.ShapeDtypeStruct((t, w), BF16),
        scratch_shapes=[pltpu.VMEM((seq_len, 1), F32), pltpu.VMEM((seq_len, HEAD_W), F32)],
        compiler_params=_cparams(("parallel", "arbitrary")),
        name="sb_sample",
    )(q, kn, vn, kc, vc)


def _resid_norm_kernel(x_ref, a_ref, gate_ref, g_ref, b_ref, shift_ref, scale_ref, x1_ref, h_ref, hf_ref, *, alpha):
    x1 = _ln(alpha * x_ref[...] + gate_ref[...] * a_ref[...]) * g_ref[...] + b_ref[...]
    x1_ref[...] = x1
    h = _ln(x1) * (1.0 + scale_ref[...]) + shift_ref[...]
    h_ref[...] = h.astype(h_ref.dtype)
    hf_ref[...] = h


def _resid_norm(x, a, mod3, row0, seq_len, g, b, alpha):
    t, d = x.shape
    tm = _tile(seq_len, 256)
    tps = seq_len // tm
    tok = pl.BlockSpec((tm, d), lambda i: (i, 0))
    vec = pl.BlockSpec((1, d), lambda i: (0, 0))
    return pl.pallas_call(
        functools.partial(_resid_norm_kernel, alpha=alpha),
        grid=(t // tm,),
        in_specs=[tok, tok, _mod_spec(d, 2, tps, row0), vec, vec,
                  _mod_spec(d, 3, tps, row0), _mod_spec(d, 4, tps, row0)],
        out_specs=[tok, tok, tok],
        out_shape=[jax.ShapeDtypeStruct((t, d), F32), jax.ShapeDtypeStruct((t, d), BF16),
                   jax.ShapeDtypeStruct((t, d), F32)],
        compiler_params=_cparams(("parallel",)),
        name="resid_norm",
    )(x, a, mod3, g, b, mod3, mod3)


def _router_kernel(h_ref, whi_ref, wlo_ref, bias_ref, gate_ref, idx_ref, rank_ref, cnt_ref, run_ref):
    @pl.when(pl.program_id(0) == 0)
    def _():
        run_ref[...] = jnp.zeros(run_ref.shape, F32)

    h = h_ref[...]
    logits = _dot(h, whi_ref[...]) + _dot(h, wlo_ref[...])
    scores = jax.nn.sigmoid(logits)
    tm, e = scores.shape
    lane = lax.broadcasted_iota(jnp.int32, (tm, e), 1)
    sel = scores + bias_ref[...]
    picked = jnp.zeros((tm, e), F32)
    idx = jnp.zeros((tm, e), jnp.int32)
    firsts = []
    for k in range(TOP_K):
        best = jnp.max(sel, axis=-1, keepdims=True)
        first = jnp.min(jnp.where(sel == best, lane, e), axis=-1, keepdims=True)
        hit = lane == first
        picked = jnp.where(hit, 1.0, picked)
        sel = jnp.where(hit, -jnp.inf, sel)
        idx = jnp.where(lane == k, first, idx)
        firsts.append(first)
    gate = scores * picked
    gate = gate / jnp.sum(gate, axis=-1, keepdims=True) * ROUTE_SCALE
    lower = (lax.broadcasted_iota(jnp.int32, (tm, tm), 1) < lax.broadcasted_iota(jnp.int32, (tm, tm), 0))
    within = _dot(jnp.where(lower, 1.0, 0.0).astype(BF16), picked.astype(BF16)) + run_ref[...]
    gate8 = jnp.zeros((tm, e), F32)
    rank8 = jnp.zeros((tm, e), F32)
    for k, first in enumerate(firsts):
        hit = lane == first
        gk = jnp.sum(jnp.where(hit, gate, 0.0), axis=-1, keepdims=True)
        rk = jnp.sum(jnp.where(hit, within, 0.0), axis=-1, keepdims=True)
        gate8 = jnp.where(lane == k, gk, gate8)
        rank8 = jnp.where(lane == k, rk, rank8)
    gate_ref[...] = gate8
    idx_ref[...] = idx
    rank_ref[...] = rank8.astype(jnp.int32)
    run_ref[...] += jnp.sum(picked, axis=0, keepdims=True)
    cnt_ref[...] = run_ref[...].astype(jnp.int32)


def _router(h, w_router, router_bias):
    t, d = h.shape
    e = w_router.shape[1]
    assert e == LANES
    tm = _tile(t, 512)
    whi = w_router.astype(BF16)
    wlo = (w_router - whi.astype(F32)).astype(BF16)
    tok = pl.BlockSpec((tm, e), lambda i: (i, 0))
    wspec = pl.BlockSpec((d, e), lambda i: (0, 0))
    one = pl.BlockSpec((1, e), lambda i: (0, 0))
    return pl.pallas_call(
        _router_kernel,
        grid=(t // tm,),
        in_specs=[pl.BlockSpec((tm, d), lambda i: (i, 0)), wspec, wspec, one],
        out_specs=[tok, tok, tok, one],
        out_shape=[jax.ShapeDtypeStruct((t, e), F32), jax.ShapeDtypeStruct((t, e), jnp.int32),
                   jax.ShapeDtypeStruct((t, e), jnp.int32), jax.ShapeDtypeStruct((1, e), jnp.int32)],
        scratch_shapes=[pltpu.VMEM((1, e), F32)],
        compiler_params=_cparams(("arbitrary",)),
        name="router",
    )(h, whi, wlo, router_bias.reshape(1, e))


def _row_copy(src_hbm, tok, buf, slot, r, sem):
    return pltpu.make_async_copy(src_hbm.at[pl.ds(tok, 1)], buf.at[slot, pl.ds(r, 1)], sem.at[slot])


def _expert_up_kernel(bexp_ref, nused_ref, tok_ref, tok_next_ref, h_hbm, wg_ref, wu_ref, o_ref, buf, sem, *, blk):
    b = pl.program_id(0)
    n_used = nused_ref[0]
    slot = lax.rem(b, 2)

    def start_gather(toks, into):
        def body(r, carry):
            _row_copy(h_hbm, toks[0, r], buf, into, r, sem).start()
            return carry
        lax.fori_loop(0, blk, body, 0)

    def wait_gather(into):
        def body(r, carry):
            _row_copy(h_hbm, 0, buf, into, r, sem).wait()
            return carry
        lax.fori_loop(0, blk, body, 0)

    @pl.when(jnp.logical_and(b == 0, n_used > 0))
    def _():
        start_gather(tok_ref, slot)

    @pl.when(b + 1 < n_used)
    def _():
        start_gather(tok_next_ref, 1 - slot)

    @pl.when(b < n_used)
    def _():
        wait_gather(slot)
        x = buf[slot].astype(BF16)
        g = _dot(x, wg_ref[...].astype(BF16))
        u = _dot(x, wu_ref[...].astype(BF16))
        o_ref[...] = (g * jax.nn.sigmoid(g) * u).astype(o_ref.dtype)

    @pl.when(b >= n_used)
    def _():
        o_ref[...] = jnp.zeros(o_ref.shape, o_ref.dtype)


def _expert_up(h, wg, wu, slot_tok, block_expert, n_used, blk):
    t, d = h.shape
    e, _, hid = wg.shape
    n_blocks = block_expert.shape[0]
    wspec = pl.BlockSpec((None, d, hid), lambda b, be, nu: (be[b], 0, 0))
    tok3 = slot_tok.reshape(n_blocks, 1, blk)
    grid_spec = pltpu.PrefetchScalarGridSpec(
        num_scalar_prefetch=2,
        grid=(n_blocks,),
        in_specs=[pl.BlockSpec((None, 1, blk), lambda b, be, nu: (b, 0, 0), memory_space=pltpu.SMEM),
                  pl.BlockSpec((None, 1, blk), lambda b, be, nu: (jnp.minimum(b + 1, n_blocks - 1), 0, 0),
                               memory_space=pltpu.SMEM),
                  pl.BlockSpec(memory_space=pl.ANY), wspec, wspec],
        out_specs=pl.BlockSpec((blk, hid), lambda b, be, nu: (b, 0)),
        scratch_shapes=[pltpu.VMEM((2, blk, d), F32), pltpu.SemaphoreType.DMA((2,))],
    )
    return pl.pallas_call(
        functools.partial(_expert_up_kernel, blk=blk),
        grid_spec=grid_spec,
        out_shape=jax.ShapeDtypeStruct((n_blocks * blk, hid), BF16),
        compiler_params=_cparams(("arbitrary",)),
        name="expert_up",
    )(block_expert, n_used, tok3, tok3, h, wg, wu)


def _expert_down_kernel(bexp_ref, nused_ref, a_ref, wd_ref, o_ref):
    @pl.when(pl.program_id(0) < nused_ref[0])
    def _():
        o_ref[...] = _dot(a_ref[...], wd_ref[...].astype(BF16))

    @pl.when(pl.program_id(0) >= nused_ref[0])
    def _():
        o_ref[...] = jnp.zeros(o_ref.shape, o_ref.dtype)


def _expert_down(act, wd, block_expert, n_used, blk):
    e, hid, d = wd.shape
    n_blocks = block_expert.shape[0]
    grid_spec = pltpu.PrefetchScalarGridSpec(
        num_scalar_prefetch=2,
        grid=(n_blocks,),
        in_specs=[pl.BlockSpec((blk, hid), lambda b, be, nu: (b, 0)),
                  pl.BlockSpec((None, hid, d), lambda b, be, nu: (be[b], 0, 0))],
        out_specs=pl.BlockSpec((blk, d), lambda b, be, nu: (b, 0)),
    )
    return pl.pallas_call(
        _expert_down_kernel,
        grid_spec=grid_spec,
        out_shape=jax.ShapeDtypeStruct((n_blocks * blk, d), F32),
        compiler_params=_cparams(("arbitrary",)),
        name="expert_down",
    )(block_expert, n_used, act, wd)


def _combine_kernel(dest_ref, y_hbm, gate_ref, sh_ref, x_ref, gf_ref, g_ref, b_ref, o_ref, buf, sem, *, tt, alpha):
    def copy(n, src_row):
        r = n // TOP_K
        k = n - r * TOP_K
        return pltpu.make_async_copy(y_hbm.at[pl.ds(src_row, 1)], buf.at[k, pl.ds(r, 1)], sem.at[0])

    def start(n, carry):
        copy(n, dest_ref[0, n]).start()
        return carry

    def wait(n, carry):
        copy(n, 0).wait()
        return carry

    lax.fori_loop(0, tt * TOP_K, start, 0)
    lax.fori_loop(0, tt * TOP_K, wait, 0)
    gate = gate_ref[...]
    f = sh_ref[...]
    for k in range(TOP_K):
        f = f + buf[k] * gate[:, k:k + 1]
    o_ref[...] = _ln(alpha * x_ref[...] + gf_ref[...] * f) * g_ref[...] + b_ref[...]


def _combine(y_slots, dest8, gate8, shared, x1, mod3, row0, seq_len, tok0, n_tok, g, b, alpha):
    d = x1.shape[1]
    tt = _tile(seq_len, 64)
    assert tok0 % tt == 0
    tile0 = tok0 // tt
    tps = seq_len // tt
    tok = pl.BlockSpec((tt, d), lambda i: (tile0 + i, 0))
    vec = pl.BlockSpec((1, d), lambda i: (0, 0))
    dest3 = dest8.reshape(-1, 1, tt * TOP_K)
    return pl.pallas_call(
        functools.partial(_combine_kernel, tt=tt, alpha=alpha),
        grid=(n_tok // tt,),
        in_specs=[pl.BlockSpec((None, 1, tt * TOP_K), lambda i: (tile0 + i, 0, 0), memory_space=pltpu.SMEM),
                  pl.BlockSpec(memory_space=pl.ANY),
                  pl.BlockSpec((tt, LANES), lambda i: (tile0 + i, 0)),
                  tok, tok,
                  pl.BlockSpec((None, 1, d), lambda i: ((row0 + i // tps) * 6 + 5, 0, 0)),
                  vec, vec],
        out_specs=pl.BlockSpec((tt, d), lambda i: (i, 0)),
        out_shape=jax.ShapeDtypeStruct((n_tok, d), F32),
        scratch_shapes=[pltpu.VMEM((TOP_K, tt, d), F32), pltpu.SemaphoreType.DMA((1,))],
        compiler_params=_cparams(("arbitrary",)),
        name="combine",
    )(dest3, y_slots, gate8, shared, x1, mod3, g, b)


def _rope_tables(pos):
    half = ROPE_DIMS // 2
    inv_freq = ROPE_THETA ** (-jnp.arange(half, dtype=F32) / half)
    ang = pos.astype(F32)[:, None] * inv_freq[None, :]
    cos, sin = jnp.cos(ang), jnp.sin(ang)
    n = pos.shape[0]
    ones = jnp.ones((n, DIFF_DK - ROPE_DIMS), F32)
    zeros = jnp.zeros((n, DIFF_DK - ROPE_DIMS), F32)
    zh = jnp.zeros((n, half), F32)
    c64 = jnp.concatenate([cos, cos, ones], axis=1)
    s1 = jnp.concatenate([-sin, zh, zeros], axis=1)
    s2 = jnp.concatenate([zh, sin, zeros], axis=1)
    rep = LANES // DIFF_DK
    return tuple(jnp.tile(a, (1, rep)) for a in (c64, s1, s2))


def _mixer(x, mod3, row0, seq_len, past, wts, lam4, g_norm, lam_init):
    (w_qd, w_kd, w_vd, w_qs, w_ks, w_vs, w_gd, w_gs, w_bd, w_bs, w_out) = wts
    t = x.shape[0]
    past_len = 0 if past is None else past[0].shape[1]
    pos = past_len + jnp.arange(seq_len, dtype=jnp.int32)
    tabs = _rope_tables(jnp.tile(pos, t // seq_len))
    h = _ln_mod(x, mod3, row0, seq_len, 0, 1)
    (qd,) = _proj(h, w_qd, (BF16,), rope_tabs=tabs, scale=DIFF_DK ** -0.5)
    (qs,) = _proj(h, w_qs, (BF16,), scale=HEAD_W ** -0.5)
    (gd,) = _proj(h, w_gd, (BF16,))
    (gs,) = _proj(h, w_gs, (BF16,))
    if past is None:
        kd, kd_b = _proj(h, w_kd, (F32, BF16), rope_tabs=tabs)
        vd, vd_b = _proj(h, w_vd, (F32, BF16))
        ks, ks_b = _proj(h, w_ks, (F32, BF16))
        vs, vs_b = _proj(h, w_vs, (F32, BF16))
        od = _diff_prompt(qd, kd_b, vd_b, lam4, g_norm, lam_init)
        osb = _sb_prompt(qs, ks_b, vs_b)
    else:
        (kd,) = _proj(h, w_kd, (F32,), rope_tabs=tabs)
        (vd,) = _proj(h, w_vd, (F32,))
        (ks,) = _proj(h, w_ks, (F32,))
        (vs,) = _proj(h, w_vs, (F32,))
        od = _diff_sample(qd, kd, vd, past[0], past[1], lam4, g_norm, lam_init, seq_len)
        osb = _sb_sample(qs, ks, vs, past[2], past[3], seq_len)
    merged = _merge(od, osb, w_bd, w_bs, gd, gs)
    (a,) = _proj(merged, w_out, (F32,))
    return a, (kd, vd, ks, vs)


def _moe_plan(idx8, rank8, counts, n_tok, blk):
    e = counts.shape[0]
    padded = (counts + blk - 1) // blk * blk
    pad_ends = jnp.cumsum(padded)
    pad_starts = pad_ends - padded
    dest8 = (pad_starts[idx8] + rank8).astype(jnp.int32)
    n_blocks = -(-n_tok * TOP_K // blk) + e
    tok_ids = jnp.repeat(jnp.arange(n_tok, dtype=jnp.int32), TOP_K)
    slot_tok = jnp.zeros((n_blocks * blk,), jnp.int32).at[dest8.reshape(-1)].set(tok_ids)
    block_pos = jnp.arange(n_blocks, dtype=jnp.int32) * blk
    block_expert = jnp.minimum(jnp.searchsorted(pad_ends, block_pos, side='right'), e - 1).astype(jnp.int32)
    n_used = (pad_ends[-1] // blk).astype(jnp.int32).reshape(1)
    return dest8, slot_tok, block_expert, n_used


def kernel(x_prompt, x_sample, c_prompt, c_sample, cache_diff_k, cache_diff_v, cache_sb_k, cache_sb_v, w_ada, b_ada, w_in, w_br_diff, w_br_sb, w_out, lam_q1, lam_k1, lam_q2, lam_k2, diff_norm_g, ln1_g, ln1_b, w_router, router_bias, w_e_gate, w_e_up, w_e_down, w_sh_gate, w_sh_up, w_sh_down, ln2_g, ln2_b):
    depth = w_ada.shape[0]
    assert depth == 1
    bp, sp, d = x_prompt.shape
    bs, ss, _ = x_sample.shape
    assert bp == 1
    diff_w = w_br_diff.shape[1]
    sb_w = w_br_sb.shape[1]
    heads_d, heads_s = diff_w // HEAD_W, sb_w // HEAD_W
    past_len = cache_diff_k.shape[2]
    alpha = (2.0 * depth) ** 0.25
    lam_init = 0.8 - 0.6 * float(np.exp(-0.3 * 0))
    blk = 256

    widths = [diff_w, diff_w, diff_w, sb_w, sb_w, sb_w, d, d]
    offs = np.concatenate([[0], np.cumsum(widths)])
    w_in0 = w_in[0]
    w_qd, w_kd, w_vd, w_qs, w_ks, w_vs, w_gd, w_gs = (
        w_in0[:, offs[i]:offs[i + 1]].astype(BF16) for i in range(8))
    wts = (w_qd, w_kd, w_vd, w_qs, w_ks, w_vs, w_gd, w_gs,
           w_br_diff[0].astype(BF16), w_br_sb[0].astype(BF16), w_out[0].astype(BF16))
    lam4 = jnp.stack([lam_q1[0], lam_k1[0], lam_q2[0], lam_k2[0]]).astype(F32)
    g_norm = diff_norm_g[0].reshape(1, HEAD_W).astype(F32)

    c_all = jnp.concatenate([c_prompt, c_sample], axis=0)
    n_seq = c_all.shape[0]
    c_pad = jnp.pad(c_all, ((0, -n_seq % 8), (0, 0)))
    mod = _ada(c_pad, w_ada[0], b_ada[0])
    mod3 = mod.reshape(c_pad.shape[0] * 6, 1, d)

    xp = x_prompt.reshape(bp * sp, d)
    xs = x_sample.reshape(bs * ss, d)
    past = (cache_diff_k[0].reshape(bs, past_len, diff_w), cache_diff_v[0].reshape(bs, past_len, diff_w),
            cache_sb_k[0].reshape(bs, past_len, sb_w), cache_sb_v[0].reshape(bs, past_len, sb_w))

    a_p, rows_p = _mixer(xp, mod3, 0, sp, None, wts, lam4, g_norm, lam_init)
    a_s, rows_s = _mixer(xs, mod3, bp, ss, past, wts, lam4, g_norm, lam_init)
    g1, b1 = ln1_g[0].reshape(1, d), ln1_b[0].reshape(1, d)
    x1_p, h2_p, h2f_p = _resid_norm(xp, a_p, mod3, 0, sp, g1, b1, alpha)
    x1_s, h2_s, h2f_s = _resid_norm(xs, a_s, mod3, bp, ss, g1, b1, alpha)

    n_p, n_s = bp * sp, bs * ss
    n_tok = n_p + n_s
    h2 = jnp.concatenate([h2_p, h2_s], axis=0)
    h2f = jnp.concatenate([h2f_p, h2f_s], axis=0)
    x1 = jnp.concatenate([x1_p, x1_s], axis=0)
    gate8, idx8, rank8, counts = _router(h2, w_router[0], router_bias[0])
    dest8, slot_tok, block_expert, n_used = _moe_plan(idx8[:, :TOP_K], rank8[:, :TOP_K], counts[0], n_tok, blk)
    act = _expert_up(h2f, w_e_gate[0], w_e_up[0], slot_tok, block_expert, n_used, blk)
    y_slots = _expert_down(act, w_e_down[0], block_expert, n_used, blk)
    sh_act = _glu(h2, w_sh_gate[0].astype(BF16), w_sh_up[0].astype(BF16))
    (shared,) = _proj(sh_act, w_sh_down[0].astype(BF16), (F32,))
    g2, b2 = ln2_g[0].reshape(1, d), ln2_b[0].reshape(1, d)
    y_p = _combine(y_slots, dest8, gate8, shared, x1, mod3, 0, sp, 0, n_p, g2, b2, alpha)
    y_s = _combine(y_slots, dest8, gate8, shared, x1, mod3, bp, ss, n_p, n_s, g2, b2, alpha)

    def rows(r, b, s):
        kd, vd, ks, vs = r
        return (kd.reshape(1, b, s, heads_d, 2, DIFF_DK), vd.reshape(1, b, s, heads_d, HEAD_W),
                ks.reshape(1, b, s, heads_s, HEAD_W), vs.reshape(1, b, s, heads_s, HEAD_W))

    return (y_p.reshape(bp, sp, d), y_s.reshape(bs, ss, d)) + rows(rows_p, bp, sp) + rows(rows_s, bs, ss)
```

```python
import functools

import jax
import jax.numpy as jnp
import numpy as np
from jax import lax
from jax.experimental import pallas as pl
from jax.experimental.pallas import tpu as pltpu

F32 = jnp.float32
BF16 = jnp.bfloat16

CHUNK = 64
DIFF_DK = 64
HEAD_W = 128
ROPE_THETA = 500000.0
ROPE_DIMS = DIFF_DK // 4
TOP_K = 8
ROUTE_SCALE = 2.5
LN_EPS = 1e-5
LANES = 128
VMEM_LIMIT = 52 * 1024 * 1024
SB_DEAD_LOG = -104.0


def _cparams(sem):
    return pltpu.CompilerParams(dimension_semantics=sem, vmem_limit_bytes=VMEM_LIMIT)


def _tile(n, pref):
    if n <= pref:
        return n
    t = pref - pref % 64
    while n % t:
        t -= 64
    assert t > 0, (n, pref)
    return t


def _ln(x):
    xc = x - jnp.mean(x, axis=-1, keepdims=True)
    return xc * lax.rsqrt(jnp.mean(xc * xc, axis=-1, keepdims=True) + LN_EPS)


def _dot(a, b):
    return jnp.dot(a, b, preferred_element_type=F32)


def _dot_t(a, b):
    return lax.dot_general(a, b, (((1,), (1,)), ((), ())), preferred_element_type=F32)


def _ada_kernel(c_ref, w_ref, b_ref, o_ref):
    c = c_ref[...]
    s = (c * jax.nn.sigmoid(c)).astype(BF16)
    o_ref[...] = _dot(s, w_ref[...].astype(BF16)) + b_ref[...]


def _ada(c, w_ada, b_ada):
    m, d = c.shape
    n = w_ada.shape[1]
    tn = _tile(n, 512)
    return pl.pallas_call(
        _ada_kernel,
        grid=(n // tn,),
        in_specs=[pl.BlockSpec((m, d), lambda j: (0, 0)),
                  pl.BlockSpec((d, tn), lambda j: (0, j)),
                  pl.BlockSpec((1, tn), lambda j: (0, j))],
        out_specs=pl.BlockSpec((m, tn), lambda j: (0, j)),
        out_shape=jax.ShapeDtypeStruct((m, n), F32),
        compiler_params=_cparams(("arbitrary",)),
        name="ada",
    )(c, w_ada, b_ada.reshape(1, n))


def _ln_mod_kernel(x_ref, shift_ref, scale_ref, h_ref):
    h_ref[...] = (_ln(x_ref[...]) * (1.0 + scale_ref[...]) + shift_ref[...]).astype(h_ref.dtype)


def _mod_spec(d, comp, tiles_per_seq, row0=0):
    return pl.BlockSpec((None, 1, d), lambda i: ((row0 + i // tiles_per_seq) * 6 + comp, 0, 0))


def _ln_mod(x, mod3, row0, seq_len, shift_comp, scale_comp):
    t, d = x.shape
    tm = _tile(seq_len, 256)
    tps = seq_len // tm
    return pl.pallas_call(
        _ln_mod_kernel,
        grid=(t // tm,),
        in_specs=[pl.BlockSpec((tm, d), lambda i: (i, 0)),
                  _mod_spec(d, shift_comp, tps, row0),
                  _mod_spec(d, scale_comp, tps, row0)],
        out_specs=pl.BlockSpec((tm, d), lambda i: (i, 0)),
        out_shape=jax.ShapeDtypeStruct((t, d), BF16),
        compiler_params=_cparams(("parallel",)),
        name="ln_mod",
    )(x, mod3, mod3)


def _proj_kernel(*refs, rope, scale, n_out):
    x_ref, w_ref = refs[0], refs[1]
    outs = refs[len(refs) - n_out:]
    acc = _dot(x_ref[...], w_ref[...])
    if rope:
        c_ref, s1_ref, s2_ref = refs[2:5]
        cos, s1, s2 = c_ref[...], s1_ref[...], s2_ref[...]
        parts = []
        for g in range(acc.shape[1] // LANES):
            blk = acc[:, g * LANES:(g + 1) * LANES]
            half = ROPE_DIMS // 2
            parts.append(blk * cos + pltpu.roll(blk, LANES - half, 1) * s1 + pltpu.roll(blk, half, 1) * s2)
        acc = jnp.concatenate(parts, axis=1) if len(parts) > 1 else parts[0]
    if scale != 1.0:
        acc = acc * scale
    for o in outs:
        o[...] = acc.astype(o.dtype)


def _proj(x, w, out_dtypes, rope_tabs=None, scale=1.0, tm_pref=512, tn_pref=1024):
    m, k = x.shape
    n = w.shape[1]
    tm, tn = _tile(m, tm_pref), _tile(n, tn_pref)
    in_specs = [pl.BlockSpec((tm, k), lambda i, j: (i, 0)),
                pl.BlockSpec((k, tn), lambda i, j: (0, j))]
    args = [x, w]
    if rope_tabs is not None:
        in_specs += [pl.BlockSpec((tm, LANES), lambda i, j: (i, 0))] * 3
        args += list(rope_tabs)
    outs = pl.pallas_call(
        functools.partial(_proj_kernel, rope=rope_tabs is not None, scale=scale, n_out=len(out_dtypes)),
        grid=(m // tm, n // tn),
        in_specs=in_specs,
        out_specs=[pl.BlockSpec((tm, tn), lambda i, j: (i, j)) for _ in out_dtypes],
        out_shape=[jax.ShapeDtypeStruct((m, n), dt) for dt in out_dtypes],
        compiler_params=_cparams(("parallel", "arbitrary")),
        name="proj",
    )(*args)
    return outs


def _glu_kernel(x_ref, wg_ref, wu_ref, o_ref):
    x = x_ref[...]
    g = _dot(x, wg_ref[...])
    u = _dot(x, wu_ref[...])
    o_ref[...] = (g * jax.nn.sigmoid(g) * u).astype(o_ref.dtype)


def _glu(x, wg, wu):
    m, k = x.shape
    n = wg.shape[1]
    tm, tn = _tile(m, 512), _tile(n, 512)
    return pl.pallas_call(
        _glu_kernel,
        grid=(m // tm, n // tn),
        in_specs=[pl.BlockSpec((tm, k), lambda i, j: (i, 0)),
                  pl.BlockSpec((k, tn), lambda i, j: (0, j)),
                  pl.BlockSpec((k, tn), lambda i, j: (0, j))],
        out_specs=pl.BlockSpec((tm, tn), lambda i, j: (i, j)),
        out_shape=jax.ShapeDtypeStruct((m, n), BF16),
        compiler_params=_cparams(("parallel", "arbitrary")),
        name="shared_glu",
    )(x, wg, wu)


def _merge_kernel(od_ref, os_ref, wd_ref, ws_ref, gd_ref, gs_ref, o_ref):
    bd = _dot(od_ref[...], wd_ref[...])
    bs = _dot(os_ref[...], ws_ref[...])
    gd = jax.nn.sigmoid(gd_ref[...].astype(F32))
    gs = jax.nn.sigmoid(gs_ref[...].astype(F32))
    o_ref[...] = (gd * bd + gs * bs).astype(o_ref.dtype)


def _merge(od, osb, wbd, wbs, gd, gs):
    m, k = od.shape
    n = wbd.shape[1]
    tm, tn = _tile(m, 512), _tile(n, 1024)
    return pl.pallas_call(
        _merge_kernel,
        grid=(m // tm, n // tn),
        in_specs=[pl.BlockSpec((tm, k), lambda i, j: (i, 0)),
                  pl.BlockSpec((tm, k), lambda i, j: (i, 0)),
                  pl.BlockSpec((k, tn), lambda i, j: (0, j)),
                  pl.BlockSpec((k, tn), lambda i, j: (0, j)),
                  pl.BlockSpec((tm, tn), lambda i, j: (i, j)),
                  pl.BlockSpec((tm, tn), lambda i, j: (i, j))],
        out_specs=pl.BlockSpec((tm, tn), lambda i, j: (i, j)),
        out_shape=jax.ShapeDtypeStruct((m, n), BF16),
        compiler_params=_cparams(("parallel", "arbitrary")),
        name="merge",
    )(od, osb, wbd, wbs, gd, gs)


def _lambda(lam_ref, lam_init):
    a = jnp.sum(lam_ref[0:1, :] * lam_ref[1:2, :], axis=-1, keepdims=True)
    b = jnp.sum(lam_ref[2:3, :] * lam_ref[3:4, :], axis=-1, keepdims=True)
    return jnp.exp(a) - jnp.exp(b) + lam_init


def _split_components(q):
    lane = lax.broadcasted_iota(jnp.int32, q.shape, 1)
    zero = jnp.zeros_like(q)
    return jnp.where(lane < DIFF_DK, q, zero), jnp.where(lane >= DIFF_DK, q, zero)


def _diff_finish(o1, l1, o2, l2, lam, g, lam_init):
    o = o1 / l1 - lam * (o2 / l2)
    o = o * lax.rsqrt(jnp.mean(o * o, axis=-1, keepdims=True) + LN_EPS)
    return o * g * (1.0 - lam_init)


def _diff_prompt_kernel(lam_ref, g_ref, q_ref, k_ref, v_ref, o_ref, m_ref, l_ref, acc_ref, *, tq, tk, lam_init):
    i = pl.program_id(1)
    q1, q2 = _split_components(q_ref[...])
    m_ref[...] = jnp.full(m_ref.shape, -jnp.inf, F32)
    l_ref[...] = jnp.zeros(l_ref.shape, F32)
    acc_ref[...] = jnp.zeros(acc_ref.shape, F32)

    def block(start, keep):
        k = k_ref[pl.ds(start, tk), :]
        v = v_ref[pl.ds(start, tk), :]
        for comp, qc in enumerate((q1, q2)):
            s = _dot_t(qc, k)
            if keep is not None:
                s = jnp.where(keep[0], jnp.where(keep[1], s, -1e30), -1e30)
            m_old = m_ref[comp]
            m_new = jnp.maximum(m_old, jnp.max(s, axis=-1, keepdims=True))
            alpha = jnp.exp(m_old - m_new)
            p = jnp.exp(s - m_new)
            l_ref[comp] = alpha * l_ref[comp] + jnp.sum(p, axis=-1, keepdims=True)
            acc_ref[comp] = alpha * acc_ref[comp] + _dot(p.astype(BF16), v)
            m_ref[comp] = m_new

    n_full = (i * tq) // tk

    def body(j, carry):
        block(pl.multiple_of(j * tk, tk), None)
        return carry

    lax.fori_loop(0, n_full, body, 0)
    start = pl.multiple_of(jnp.maximum((i + 1) * tq - tk, 0), tq)
    lo = n_full * tk
    col = start + lax.broadcasted_iota(jnp.int32, (tq, tk), 1)
    row = i * tq + lax.broadcasted_iota(jnp.int32, (tq, 1), 0)
    hi = (row // CHUNK + 1) * CHUNK
    block(start, (col >= lo, col < hi))
    lam = _lambda(lam_ref, lam_init)
    o = _diff_finish(acc_ref[0], l_ref[0], acc_ref[1], l_ref[1], lam, g_ref[...], lam_init)
    o_ref[...] = o.astype(o_ref.dtype)


def _diff_prompt(q, k, v, lam4, g, lam_init):
    t, w = q.shape
    heads = w // HEAD_W
    tq = _tile(t, 256)
    tk = _tile(t, 1024)
    assert tq % CHUNK == 0 and tk % tq == 0
    return pl.pallas_call(
        functools.partial(_diff_prompt_kernel, tq=tq, tk=tk, lam_init=lam_init),
        grid=(heads, t // tq),
        in_specs=[pl.BlockSpec((4, DIFF_DK), lambda h, i: (0, 0)),
                  pl.BlockSpec((1, HEAD_W), lambda h, i: (0, 0)),
                  pl.BlockSpec((tq, HEAD_W), lambda h, i: (i, h)),
                  pl.BlockSpec((t, HEAD_W), lambda h, i: (0, h)),
                  pl.BlockSpec((t, HEAD_W), lambda h, i: (0, h))],
        out_specs=pl.BlockSpec((tq, HEAD_W), lambda h, i: (i, h)),
        out_shape=jax.ShapeDtypeStruct((t, w), BF16),
        scratch_shapes=[pltpu.VMEM((2, tq, 1), F32), pltpu.VMEM((2, tq, 1), F32),
                        pltpu.VMEM((2, tq, HEAD_W), F32)],
        compiler_params=_cparams(("parallel", "arbitrary")),
        name="diff_prompt",
    )(lam4, g, q, k, v)


def _diff_sample_kernel(lam_ref, g_ref, q_ref, kn_ref, vn_ref, kc_ref, vc_ref, o_ref, *, past_len, lam_init):
    q1, q2 = _split_components(q_ref[...])
    kp = kc_ref[...].astype(BF16)
    vp = vc_ref[...].astype(BF16)
    kn = kn_ref[...].astype(BF16)
    vn = vn_ref[...].astype(BF16)
    tn = kn.shape[0]
    r = (past_len + lax.broadcasted_iota(jnp.int32, (tn, tn), 0)) // CHUNK
    c = (past_len + lax.broadcasted_iota(jnp.int32, (tn, tn), 1)) // CHUNK
    keep = c <= r
    res = []
    for qc in (q1, q2):
        sp = _dot_t(qc, kp)
        sn = jnp.where(keep, _dot_t(qc, kn), -1e30)
        m = jnp.maximum(jnp.max(sp, axis=-1, keepdims=True), jnp.max(sn, axis=-1, keepdims=True))
        pp = jnp.exp(sp - m)
        pn = jnp.exp(sn - m)
        l = jnp.sum(pp, axis=-1, keepdims=True) + jnp.sum(pn, axis=-1, keepdims=True)
        res.append((_dot(pp.astype(BF16), vp) + _dot(pn.astype(BF16), vn), l))
    lam = _lambda(lam_ref, lam_init)
    o = _diff_finish(res[0][0], res[0][1], res[1][0], res[1][1], lam, g_ref[...], lam_init)
    o_ref[...] = o.astype(o_ref.dtype)


def _sample_specs(seq_len, past_len, heads):
    row = pl.BlockSpec((seq_len, HEAD_W), lambda bi, h: (bi, h))
    cache = pl.BlockSpec((None, past_len, HEAD_W), lambda bi, h: (bi, 0, h))
    return row, cache


def _diff_sample(q, kn, vn, kc, vc, lam4, g, lam_init, seq_len):
    t, w = q.shape
    b, past_len, _ = kc.shape
    heads = w // HEAD_W
    assert past_len % CHUNK == 0
    row, cache = _sample_specs(seq_len, past_len, heads)
    return pl.pallas_call(
        functools.partial(_diff_sample_kernel, past_len=past_len, lam_init=lam_init),
        grid=(b, heads),
        in_specs=[pl.BlockSpec((4, DIFF_DK), lambda bi, h: (0, 0)),
                  pl.BlockSpec((1, HEAD_W), lambda bi, h: (0, 0)),
                  row, row, row, cache, cache],
        out_specs=row,
        out_shape=jax.ShapeDtypeStruct((t, w), BF16),
        compiler_params=_cparams(("parallel", "arbitrary")),
        name="diff_sample",
    )(lam4, g, q, kn, vn, kc, vc)


def _sb_block(q, k, v, carry_ref, acc_ref, keep):
    tk = k.shape[0]
    z = _dot_t(q, k)
    soft = jnp.log(1.0 + jnp.exp(-jnp.abs(z)))
    log_sig = jnp.minimum(z, 0.0) - soft
    log_keep = jnp.minimum(-z, 0.0) - soft
    if keep is not None:
        log_keep = jnp.where(keep, log_keep, 0.0)
    upper = (lax.broadcasted_iota(jnp.int32, (tk, tk), 0) > lax.broadcasted_iota(jnp.int32, (tk, tk), 1))
    upper = jnp.where(upper, 1.0, 0.0).astype(BF16)
    hi = log_keep.astype(BF16)
    lo = (log_keep - hi.astype(F32)).astype(BF16)
    later = _dot(hi, upper) + _dot(lo, upper) + carry_ref[...]
    w = jnp.exp(log_sig + later)
    if keep is not None:
        w = jnp.where(keep, w, 0.0)
    acc_ref[...] += _dot(w.astype(BF16), v)
    carry_ref[...] += jnp.sum(log_keep, axis=-1, keepdims=True)


def _sb_prompt_kernel(q_ref, k_ref, v_ref, o_ref, carry_ref, acc_ref, *, tq):
    i = pl.program_id(1)
    q = q_ref[...]
    carry_ref[...] = jnp.zeros(carry_ref.shape, F32)
    acc_ref[...] = jnp.zeros(acc_ref.shape, F32)

    def load(j):
        start = pl.multiple_of(j * tq, tq)
        return k_ref[pl.ds(start, tq), :], v_ref[pl.ds(start, tq), :]

    strict = (lax.broadcasted_iota(jnp.int32, (tq, tq), 1) < lax.broadcasted_iota(jnp.int32, (tq, tq), 0))
    k, v = load(i)
    _sb_block(q, k, v, carry_ref, acc_ref, strict)

    def cond(state):
        j, alive = state
        return jnp.logical_and(j >= 0, alive > SB_DEAD_LOG)

    def body(state):
        j, _ = state
        kj, vj = load(j)
        _sb_block(q, kj, vj, carry_ref, acc_ref, None)
        return j - 1, jnp.max(carry_ref[...])

    lax.while_loop(cond, body, (i - 1, jnp.max(carry_ref[...])))
    o_ref[...] = acc_ref[...].astype(o_ref.dtype)


def _sb_prompt(q, k, v):
    t, w = q.shape
    heads = w // HEAD_W
    tq = _tile(t, 256)
    return pl.pallas_call(
        functools.partial(_sb_prompt_kernel, tq=tq),
        grid=(heads, t // tq),
        in_specs=[pl.BlockSpec((tq, HEAD_W), lambda h, i: (i, h)),
                  pl.BlockSpec((t, HEAD_W), lambda h, i: (0, h)),
                  pl.BlockSpec((t, HEAD_W), lambda h, i: (0, h))],
        out_specs=pl.BlockSpec((tq, HEAD_W), lambda h, i: (i, h)),
        out_shape=jax.ShapeDtypeStruct((t, w), BF16),
        scratch_shapes=[pltpu.VMEM((tq, 1), F32), pltpu.VMEM((tq, HEAD_W), F32)],
        compiler_params=_cparams(("parallel", "arbitrary")),
        name="sb_prompt",
    )(q, k, v)


def _sb_sample_kernel(q_ref, kn_ref, vn_ref, kc_ref, vc_ref, o_ref, carry_ref, acc_ref, *, tk):
    q = q_ref[...]
    carry_ref[...] = jnp.zeros(carry_ref.shape, F32)
    acc_ref[...] = jnp.zeros(acc_ref.shape, F32)
    tn = kn_ref.shape[0]
    strict = (lax.broadcasted_iota(jnp.int32, (tn, tn), 1) < lax.broadcasted_iota(jnp.int32, (tn, tn), 0))
    _sb_block(q, kn_ref[...].astype(BF16), vn_ref[...].astype(BF16), carry_ref, acc_ref, strict)
    past_len = kc_ref.shape[0]
    for j in reversed(range(past_len // tk)):
        @pl.when(jnp.max(carry_ref[...]) > SB_DEAD_LOG)
        def _():
            kj = kc_ref[j * tk:(j + 1) * tk, :].astype(BF16)
            vj = vc_ref[j * tk:(j + 1) * tk, :].astype(BF16)
            _sb_block(q, kj, vj, carry_ref, acc_ref, None)
    o_ref[...] = acc_ref[...].astype(o_ref.dtype)


def _sb_sample(q, kn, vn, kc, vc, seq_len):
    t, w = q.shape
    b, past_len, _ = kc.shape
    heads = w // HEAD_W
    tk = _tile(past_len, 256)
    row, cache = _sample_specs(seq_len, past_len, heads)
    return pl.pallas_call(
        functools.partial(_sb_sample_kernel, tk=tk),
        grid=(b, heads),
        in_specs=[row, row, row, cache, cache],
        out_specs=row,
        out_shape=jax.ShapeDtypeStruct((t, w), BF16),
        scratch_shapes=[pltpu.VMEM((seq_len, 1), F32), pltpu.VMEM((seq_len, HEAD_W), F32)],
        compiler_params=_cparams(("parallel", "arbitrary")),
        name="sb_sample",
    )(q, kn, vn, kc, vc)


def _resid_norm_kernel(x_ref, a_ref, gate_ref, g_ref, b_ref, shift_ref, scale_ref, x1_ref, h_ref, hf_ref, *, alpha):
    x1 = _ln(alpha * x_ref[...] + gate_ref[...] * a_ref[...]) * g_ref[...] + b_ref[...]
    x1_ref[...] = x1
    h = _ln(x1) * (1.0 + scale_ref[...]) + shift_ref[...]
    h_ref[...] = h.astype(h_ref.dtype)
    hf_ref[...] = h


def _resid_norm(x, a, mod3, row0, seq_len, g, b, alpha):
    t, d = x.shape
    tm = _tile(seq_len, 256)
    tps = seq_len // tm
    tok = pl.BlockSpec((tm, d), lambda i: (i, 0))
    vec = pl.BlockSpec((1, d), lambda i: (0, 0))
    return pl.pallas_call(
        functools.partial(_resid_norm_kernel, alpha=alpha),
        grid=(t // tm,),
        in_specs=[tok, tok, _mod_spec(d, 2, tps, row0), vec, vec,
                  _mod_spec(d, 3, tps, row0), _mod_spec(d, 4, tps, row0)],
        out_specs=[tok, tok, tok],
        out_shape=[jax.ShapeDtypeStruct((t, d), F32), jax.ShapeDtypeStruct((t, d), BF16),
                   jax.ShapeDtypeStruct((t, d), F32)],
        compiler_params=_cparams(("parallel",)),
        name="resid_norm",
    )(x, a, mod3, g, b, mod3, mod3)


def _router_kernel(h_ref, whi_ref, wlo_ref, bias_ref, gate_ref, idx_ref, rank_ref, cnt_ref, run_ref):
    @pl.when(pl.program_id(0) == 0)
    def _():
        run_ref[...] = jnp.zeros(run_ref.shape, F32)

    h = h_ref[...]
    logits = _dot(h, whi_ref[...]) + _dot(h, wlo_ref[...])
    scores = jax.nn.sigmoid(logits)
    tm, e = scores.shape
    lane = lax.broadcasted_iota(jnp.int32, (tm, e), 1)
    sel = scores + bias_ref[...]
    picked = jnp.zeros((tm, e), F32)
    idx = jnp.zeros((tm, e), jnp.int32)
    firsts = []
    for k in range(TOP_K):
        best = jnp.max(sel, axis=-1, keepdims=True)
        first = jnp.min(jnp.where(sel == best, lane, e), axis=-1, keepdims=True)
        hit = lane == first
        picked = jnp.where(hit, 1.0, picked)
        sel = jnp.where(hit, -jnp.inf, sel)
        idx = jnp.where(lane == k, first, idx)
        firsts.append(first)
    gate = scores * picked
    gate = gate / jnp.sum(gate, axis=-1, keepdims=True) * ROUTE_SCALE
    lower = (lax.broadcasted_iota(jnp.int32, (tm, tm), 1) < lax.broadcasted_iota(jnp.int32, (tm, tm), 0))
    within = _dot(jnp.where(lower, 1.0, 0.0).astype(BF16), picked.astype(BF16)) + run_ref[...]
    gate8 = jnp.zeros((tm, e), F32)
    rank8 = jnp.zeros((tm, e), F32)
    for k, first in enumerate(firsts):
        hit = lane == first
        gk = jnp.sum(jnp.where(hit, gate, 0.0), axis=-1, keepdims=True)
        rk = jnp.sum(jnp.where(hit, within, 0.0), axis=-1, keepdims=True)
        gate8 = jnp.where(lane == k, gk, gate8)
        rank8 = jnp.where(lane == k, rk, rank8)
    gate_ref[...] = gate8
    idx_ref[...] = idx
    rank_ref[...] = rank8.astype(jnp.int32)
    run_ref[...] += jnp.sum(picked, axis=0, keepdims=True)
    cnt_ref[...] = run_ref[...].astype(jnp.int32)


def _router(h, w_router, router_bias):
    t, d = h.shape
    e = w_router.shape[1]
    assert e == LANES
    tm = _tile(t, 512)
    whi = w_router.astype(BF16)
    wlo = (w_router - whi.astype(F32)).astype(BF16)
    tok = pl.BlockSpec((tm, e), lambda i: (i, 0))
    wspec = pl.BlockSpec((d, e), lambda i: (0, 0))
    one = pl.BlockSpec((1, e), lambda i: (0, 0))
    return pl.pallas_call(
        _router_kernel,
        grid=(t // tm,),
        in_specs=[pl.BlockSpec((tm, d), lambda i: (i, 0)), wspec, wspec, one],
        out_specs=[tok, tok, tok, one],
        out_shape=[jax.ShapeDtypeStruct((t, e), F32), jax.ShapeDtypeStruct((t, e), jnp.int32),
                   jax.ShapeDtypeStruct((t, e), jnp.int32), jax.ShapeDtypeStruct((1, e), jnp.int32)],
        scratch_shapes=[pltpu.VMEM((1, e), F32)],
        compiler_params=_cparams(("arbitrary",)),
        name="router",
    )(h, whi, wlo, router_bias.reshape(1, e))


def _row_copy(src_hbm, tok, buf, slot, r, sem):
    return pltpu.make_async_copy(src_hbm.at[pl.ds(tok, 1)], buf.at[slot, pl.ds(r, 1)], sem.at[slot])


def _expert_up_kernel(bexp_ref, nused_ref, tok_ref, tok_next_ref, h_hbm, wg_ref, wu_ref, o_ref, buf, sem, *,
                      blk, n_blocks):
    b = pl.program_id(0)
    n_used = nused_ref[0]
    slot = lax.rem(b, 2)

    def start_rows(toks, into):
        for r in range(blk):
            _row_copy(h_hbm, toks[0, r], buf, into, r, sem).start()

    def wait_rows(into):
        for r in range(blk):
            _row_copy(h_hbm, 0, buf, into, r, sem).wait()

    @pl.when(b == 0)
    def _():
        def body(r, carry):
            _row_copy(h_hbm, tok_ref[0, r], buf, slot, r, sem).start()
            return carry
        lax.fori_loop(0, blk, body, 0)

    @pl.when(b < n_used)
    def _():
        wait_rows(slot)
        start_rows(tok_next_ref, 1 - slot)
        x = buf[slot].astype(BF16)
        g = _dot(x, wg_ref[...].astype(BF16))
        u = _dot(x, wu_ref[...].astype(BF16))
        o_ref[...] = (g * jax.nn.sigmoid(g) * u).astype(o_ref.dtype)

        @pl.when(b == n_blocks - 1)
        def _():
            wait_rows(1 - slot)

    @pl.when(b == n_used)
    def _():
        wait_rows(slot)

    @pl.when(b >= n_used)
    def _():
        o_ref[...] = jnp.zeros(o_ref.shape, o_ref.dtype)


def _expert_up(h, wg, wu, slot_tok, block_expert, n_used, blk):
    t, d = h.shape
    e, _, hid = wg.shape
    n_blocks = block_expert.shape[0]
    wspec = pl.BlockSpec((None, d, hid), lambda b, be, nu: (be[b], 0, 0))
    tok3 = slot_tok.reshape(n_blocks, 1, blk)
    grid_spec = pltpu.PrefetchScalarGridSpec(
        num_scalar_prefetch=2,
        grid=(n_blocks,),
        in_specs=[pl.BlockSpec((None, 1, blk), lambda b, be, nu: (b, 0, 0), memory_space=pltpu.SMEM),
                  pl.BlockSpec((None, 1, blk), lambda b, be, nu: (jnp.minimum(b + 1, n_blocks - 1), 0, 0),
                               memory_space=pltpu.SMEM),
                  pl.BlockSpec(memory_space=pl.ANY), wspec, wspec],
        out_specs=pl.BlockSpec((blk, hid), lambda b, be, nu: (b, 0)),
        scratch_shapes=[pltpu.VMEM((2, blk, d), F32), pltpu.SemaphoreType.DMA((2,))],
    )
    return pl.pallas_call(
        functools.partial(_expert_up_kernel, blk=blk, n_blocks=n_blocks),
        grid_spec=grid_spec,
        out_shape=jax.ShapeDtypeStruct((n_blocks * blk, hid), BF16),
        compiler_params=_cparams(("arbitrary",)),
        name="expert_up",
    )(block_expert, n_used, tok3, tok3, h, wg, wu)


def _expert_down_kernel(bexp_ref, nused_ref, a_ref, wd_ref, o_ref):
    @pl.when(pl.program_id(0) < nused_ref[0])
    def _():
        o_ref[...] = _dot(a_ref[...], wd_ref[...].astype(BF16))

    @pl.when(pl.program_id(0) >= nused_ref[0])
    def _():
        o_ref[...] = jnp.zeros(o_ref.shape, o_ref.dtype)


def _expert_down(act, wd, block_expert, n_used, blk):
    e, hid, d = wd.shape
    n_blocks = block_expert.shape[0]
    grid_spec = pltpu.PrefetchScalarGridSpec(
        num_scalar_prefetch=2,
        grid=(n_blocks,),
        in_specs=[pl.BlockSpec((blk, hid), lambda b, be, nu: (b, 0)),
                  pl.BlockSpec((None, hid, d), lambda b, be, nu: (be[b], 0, 0))],
        out_specs=pl.BlockSpec((blk, d), lambda b, be, nu: (b, 0)),
    )
    return pl.pallas_call(
        _expert_down_kernel,
        grid_spec=grid_spec,
        out_shape=jax.ShapeDtypeStruct((n_blocks * blk, d), F32),
        compiler_params=_cparams(("arbitrary",)),
        name="expert_down",
    )(block_expert, n_used, act, wd)


def _combine_kernel(dest_ref, dest_next_ref, y_hbm, gate_ref, sh_ref, x_ref, gf_ref, g_ref, b_ref, o_ref, buf, sem, *,
                    tt, n_steps, alpha):
    i = pl.program_id(0)
    slot = lax.rem(i, 2)

    def copy(src_row, into, r, k):
        return pltpu.make_async_copy(y_hbm.at[pl.ds(src_row, 1)], buf.at[into, k, pl.ds(r, 1)], sem.at[into])

    def start_rows(dref, into):
        for r in range(tt):
            for k in range(TOP_K):
                copy(dref[0, r * TOP_K + k], into, r, k).start()

    def wait_rows(into):
        for r in range(tt):
            for k in range(TOP_K):
                copy(0, into, r, k).wait()

    @pl.when(i == 0)
    def _():
        def body(r, carry):
            for k in range(TOP_K):
                copy(dest_ref[0, r * TOP_K + k], slot, r, k).start()
            return carry
        lax.fori_loop(0, tt, body, 0)

    wait_rows(slot)
    start_rows(dest_next_ref, 1 - slot)
    gate = gate_ref[...]
    f = sh_ref[...]
    for k in range(TOP_K):
        f = f + buf[slot, k] * gate[:, k:k + 1]
    o_ref[...] = _ln(alpha * x_ref[...] + gf_ref[...] * f) * g_ref[...] + b_ref[...]

    @pl.when(i == n_steps - 1)
    def _():
        wait_rows(1 - slot)


def _combine(y_slots, dest8, gate8, shared, x1, mod3, row0, seq_len, tok0, n_tok, g, b, alpha):
    d = x1.shape[1]
    tt = _tile(seq_len, 64)
    assert tok0 % tt == 0
    tile0 = tok0 // tt
    tps = seq_len // tt
    n_steps = n_tok // tt
    tok = pl.BlockSpec((tt, d), lambda i: (tile0 + i, 0))
    vec = pl.BlockSpec((1, d), lambda i: (0, 0))
    dest3 = dest8.reshape(-1, 1, tt * TOP_K)
    return pl.pallas_call(
        functools.partial(_combine_kernel, tt=tt, n_steps=n_steps, alpha=alpha),
        grid=(n_steps,),
        in_specs=[pl.BlockSpec((None, 1, tt * TOP_K), lambda i: (tile0 + i, 0, 0), memory_space=pltpu.SMEM),
                  pl.BlockSpec((None, 1, tt * TOP_K), lambda i: (tile0 + jnp.minimum(i + 1, n_steps - 1), 0, 0),
                               memory_space=pltpu.SMEM),
                  pl.BlockSpec(memory_space=pl.ANY),
                  pl.BlockSpec((tt, LANES), lambda i: (tile0 + i, 0)),
                  tok, tok,
                  pl.BlockSpec((None, 1, d), lambda i: ((row0 + i // tps) * 6 + 5, 0, 0)),
                  vec, vec],
        out_specs=pl.BlockSpec((tt, d), lambda i: (i, 0)),
        out_shape=jax.ShapeDtypeStruct((n_tok, d), F32),
        scratch_shapes=[pltpu.VMEM((2, TOP_K, tt, d), F32), pltpu.SemaphoreType.DMA((2,))],
        compiler_params=_cparams(("arbitrary",)),
        name="combine",
    )(dest3, dest3, y_slots, gate8, shared, x1, mod3, g, b)


def _rope_tables(pos):
    half = ROPE_DIMS // 2
    inv_freq = ROPE_THETA ** (-jnp.arange(half, dtype=F32) / half)
    ang = pos.astype(F32)[:, None] * inv_freq[None, :]
    cos, sin = jnp.cos(ang), jnp.sin(ang)
    n = pos.shape[0]
    ones = jnp.ones((n, DIFF_DK - ROPE_DIMS), F32)
    zeros = jnp.zeros((n, DIFF_DK - ROPE_DIMS), F32)
    zh = jnp.zeros((n, half), F32)
    c64 = jnp.concatenate([cos, cos, ones], axis=1)
    s1 = jnp.concatenate([-sin, zh, zeros], axis=1)
    s2 = jnp.concatenate([zh, sin, zeros], axis=1)
    rep = LANES // DIFF_DK
    return tuple(jnp.tile(a, (1, rep)) for a in (c64, s1, s2))


def _mixer(x, mod3, row0, seq_len, past, wts, lam4, g_norm, lam_init):
    (w_qd, w_kd, w_vd, w_qs, w_ks, w_vs, w_gd, w_gs, w_bd, w_bs, w_out) = wts
    t = x.shape[0]
    past_len = 0 if past is None else past[0].shape[1]
    pos = past_len + jnp.arange(seq_len, dtype=jnp.int32)
    tabs = _rope_tables(jnp.tile(pos, t // seq_len))
    h = _ln_mod(x, mod3, row0, seq_len, 0, 1)
    (qd,) = _proj(h, w_qd, (BF16,), rope_tabs=tabs, scale=DIFF_DK ** -0.5)
    (qs,) = _proj(h, w_qs, (BF16,), scale=HEAD_W ** -0.5)
    (gd,) = _proj(h, w_gd, (BF16,))
    (gs,) = _proj(h, w_gs, (BF16,))
    if past is None:
        kd, kd_b = _proj(h, w_kd, (F32, BF16), rope_tabs=tabs)
        vd, vd_b = _proj(h, w_vd, (F32, BF16))
        ks, ks_b = _proj(h, w_ks, (F32, BF16))
        vs, vs_b = _proj(h, w_vs, (F32, BF16))
        od = _diff_prompt(qd, kd_b, vd_b, lam4, g_norm, lam_init)
        osb = _sb_prompt(qs, ks_b, vs_b)
    else:
        (kd,) = _proj(h, w_kd, (F32,), rope_tabs=tabs)
        (vd,) = _proj(h, w_vd, (F32,))
        (ks,) = _proj(h, w_ks, (F32,))
        (vs,) = _proj(h, w_vs, (F32,))
        od = _diff_sample(qd, kd, vd, past[0], past[1], lam4, g_norm, lam_init, seq_len)
        osb = _sb_sample(qs, ks, vs, past[2], past[3], seq_len)
    merged = _merge(od, osb, w_bd, w_bs, gd, gs)
    (a,) = _proj(merged, w_out, (F32,))
    return a, (kd, vd, ks, vs)


def _moe_plan(idx8, rank8, counts, n_tok, blk):
    e = counts.shape[0]
    padded = (counts + blk - 1) // blk * blk
    pad_ends = jnp.cumsum(padded)
    pad_starts = pad_ends - padded
    hit = idx8[..., None] == jnp.arange(e, dtype=jnp.int32)
    dest8 = (jnp.sum(jnp.where(hit, pad_starts, 0), axis=-1) + rank8).astype(jnp.int32)
    n_blocks = -(-n_tok * TOP_K // blk) + e
    tok_ids = jnp.repeat(jnp.arange(n_tok, dtype=jnp.int32), TOP_K)
    slot_tok = jnp.zeros((n_blocks * blk,), jnp.int32).at[dest8.reshape(-1)].set(tok_ids)
    block_pos = jnp.arange(n_blocks, dtype=jnp.int32) * blk
    block_expert = jnp.minimum(jnp.searchsorted(pad_ends, block_pos, side='right'), e - 1).astype(jnp.int32)
    n_used = (pad_ends[-1] // blk).astype(jnp.int32).reshape(1)
    return dest8, slot_tok, block_expert, n_used


def kernel(x_prompt, x_sample, c_prompt, c_sample, cache_diff_k, cache_diff_v, cache_sb_k, cache_sb_v, w_ada, b_ada, w_in, w_br_diff, w_br_sb, w_out, lam_q1, lam_k1, lam_q2, lam_k2, diff_norm_g, ln1_g, ln1_b, w_router, router_bias, w_e_gate, w_e_up, w_e_down, w_sh_gate, w_sh_up, w_sh_down, ln2_g, ln2_b):
    depth = w_ada.shape[0]
    assert depth == 1
    bp, sp, d = x_prompt.shape
    bs, ss, _ = x_sample.shape
    assert bp == 1
    diff_w = w_br_diff.shape[1]
    sb_w = w_br_sb.shape[1]
    heads_d, heads_s = diff_w // HEAD_W, sb_w // HEAD_W
    past_len = cache_diff_k.shape[2]
    alpha = (2.0 * depth) ** 0.25
    lam_init = 0.8 - 0.6 * float(np.exp(-0.3 * 0))
    blk = 256

    widths = [diff_w, diff_w, diff_w, sb_w, sb_w, sb_w, d, d]
    offs = np.concatenate([[0], np.cumsum(widths)])
    w_in0 = w_in[0]
    w_qd, w_kd, w_vd, w_qs, w_ks, w_vs, w_gd, w_gs = (
        w_in0[:, offs[i]:offs[i + 1]].astype(BF16) for i in range(8))
    wts = (w_qd, w_kd, w_vd, w_qs, w_ks, w_vs, w_gd, w_gs,
           w_br_diff[0].astype(BF16), w_br_sb[0].astype(BF16), w_out[0].astype(BF16))
    lam4 = jnp.stack([lam_q1[0], lam_k1[0], lam_q2[0], lam_k2[0]]).astype(F32)
    g_norm = diff_norm_g[0].reshape(1, HEAD_W).astype(F32)

    c_all = jnp.concatenate([c_prompt, c_sample], axis=0)
    n_seq = c_all.shape[0]
    c_pad = jnp.pad(c_all, ((0, -n_seq % 8), (0, 0)))
    mod = _ada(c_pad, w_ada[0], b_ada[0])
    mod3 = mod.reshape(c_pad.shape[0] * 6, 1, d)

    xp = x_prompt.reshape(bp * sp, d)
    xs = x_sample.reshape(bs * ss, d)
    past = (cache_diff_k[0].reshape(bs, past_len, diff_w), cache_diff_v[0].reshape(bs, past_len, diff_w),
            cache_sb_k[0].reshape(bs, past_len, sb_w), cache_sb_v[0].reshape(bs, past_len, sb_w))

    a_p, rows_p = _mixer(xp, mod3, 0, sp, None, wts, lam4, g_norm, lam_init)
    a_s, rows_s = _mixer(xs, mod3, bp, ss, past, wts, lam4, g_norm, lam_init)
    g1, b1 = ln1_g[0].reshape(1, d), ln1_b[0].reshape(1, d)
    x1_p, h2_p, h2f_p = _resid_norm(xp, a_p, mod3, 0, sp, g1, b1, alpha)
    x1_s, h2_s, h2f_s = _resid_norm(xs, a_s, mod3, bp, ss, g1, b1, alpha)

    n_p, n_s = bp * sp, bs * ss
    n_tok = n_p + n_s
    h2 = jnp.concatenate([h2_p, h2_s], axis=0)
    h2f = jnp.concatenate([h2f_p, h2f_s], axis=0)
    x1 = jnp.concatenate([x1_p, x1_s], axis=0)
    gate8, idx8, rank8, counts = _router(h2, w_router[0], router_bias[0])
    dest8, slot_tok, block_expert, n_used = _moe_plan(idx8[:, :TOP_K], rank8[:, :TOP_K], counts[0], n_tok, blk)
    act = _expert_up(h2f, w_e_gate[0], w_e_up[0], slot_tok, block_expert, n_used, blk)
    y_slots = _expert_down(act, w_e_down[0], block_expert, n_used, blk)
    sh_act = _glu(h2, w_sh_gate[0].astype(BF16), w_sh_up[0].astype(BF16))
    (shared,) = _proj(sh_act, w_sh_down[0].astype(BF16), (F32,))
    g2, b2 = ln2_g[0].reshape(1, d), ln2_b[0].reshape(1, d)
    y_p = _combine(y_slots, dest8, gate8, shared, x1, mod3, 0, sp, 0, n_p, g2, b2, alpha)
    y_s = _combine(y_slots, dest8, gate8, shared, x1, mod3, bp, ss, n_p, n_s, g2, b2, alpha)

    def rows(r, b, s):
        kd, vd, ks, vs = r
        return (kd.reshape(1, b, s, heads_d, 2, DIFF_DK), vd.reshape(1, b, s, heads_d, HEAD_W),
                ks.reshape(1, b, s, heads_s, HEAD_W), vs.reshape(1, b, s, heads_s, HEAD_W))

    return (y_p.reshape(bp, sp, d), y_s.reshape(bs, ss, d)) + rows(rows_p, bp, sp) + rows(rows_s, bs, ss)
```

```python
import functools

import jax
import jax.numpy as jnp
import numpy as np
from jax import lax
from jax.experimental import pallas as pl
from jax.experimental.pallas import tpu as pltpu

F32 = jnp.float32
BF16 = jnp.bfloat16

CHUNK = 64
DIFF_DK = 64
HEAD_W = 128
ROPE_THETA = 500000.0
ROPE_DIMS = DIFF_DK // 4
TOP_K = 8
ROUTE_SCALE = 2.5
LN_EPS = 1e-5
LANES = 128
VMEM_LIMIT = 52 * 1024 * 1024
SB_DEAD_LOG = -104.0
SAMPLE_PAST_CHUNK = 1024


def _cparams(sem):
    return pltpu.CompilerParams(dimension_semantics=sem, vmem_limit_bytes=VMEM_LIMIT)


def _tile(n, pref):
    if n <= pref:
        return n
    t = pref - pref % 64
    while n % t:
        t -= 64
    assert t > 0, (n, pref)
    return t


def _ln(x):
    xc = x - jnp.mean(x, axis=-1, keepdims=True)
    return xc * lax.rsqrt(jnp.mean(xc * xc, axis=-1, keepdims=True) + LN_EPS)


def _dot(a, b):
    return jnp.dot(a, b, preferred_element_type=F32)


def _dot_t(a, b):
    return lax.dot_general(a, b, (((1,), (1,)), ((), ())), preferred_element_type=F32)


def _ada_kernel(c_ref, w_ref, b_ref, o_ref):
    c = c_ref[...]
    s = (c * jax.nn.sigmoid(c)).astype(BF16)
    o_ref[...] = _dot(s, w_ref[...].astype(BF16)) + b_ref[...]


def _ada(c, w_ada, b_ada):
    m, d = c.shape
    n = w_ada.shape[1]
    tn = _tile(n, 512)
    return pl.pallas_call(
        _ada_kernel,
        grid=(n // tn,),
        in_specs=[pl.BlockSpec((m, d), lambda j: (0, 0)),
                  pl.BlockSpec((d, tn), lambda j: (0, j)),
                  pl.BlockSpec((1, tn), lambda j: (0, j))],
        out_specs=pl.BlockSpec((m, tn), lambda j: (0, j)),
        out_shape=jax.ShapeDtypeStruct((m, n), F32),
        compiler_params=_cparams(("arbitrary",)),
        name="ada",
    )(c, w_ada, b_ada.reshape(1, n))


def _ln_mod_kernel(x_ref, shift_ref, scale_ref, h_ref):
    h_ref[...] = (_ln(x_ref[...]) * (1.0 + scale_ref[...]) + shift_ref[...]).astype(h_ref.dtype)


def _mod_spec(d, comp, tiles_per_seq, row0=0):
    return pl.BlockSpec((None, 1, d), lambda i: ((row0 + i // tiles_per_seq) * 6 + comp, 0, 0))


def _ln_mod(x, mod3, row0, seq_len, shift_comp, scale_comp):
    t, d = x.shape
    tm = _tile(seq_len, 256)
    tps = seq_len // tm
    return pl.pallas_call(
        _ln_mod_kernel,
        grid=(t // tm,),
        in_specs=[pl.BlockSpec((tm, d), lambda i: (i, 0)),
                  _mod_spec(d, shift_comp, tps, row0),
                  _mod_spec(d, scale_comp, tps, row0)],
        out_specs=pl.BlockSpec((tm, d), lambda i: (i, 0)),
        out_shape=jax.ShapeDtypeStruct((t, d), BF16),
        compiler_params=_cparams(("parallel",)),
        name="ln_mod",
    )(x, mod3, mod3)


def _proj_kernel(*refs, rope, scale, n_out):
    x_ref, w_ref = refs[0], refs[1]
    outs = refs[len(refs) - n_out:]
    acc = _dot(x_ref[...], w_ref[...])
    if rope:
        c_ref, s1_ref, s2_ref = refs[2:5]
        cos, s1, s2 = c_ref[...], s1_ref[...], s2_ref[...]
        parts = []
        for g in range(acc.shape[1] // LANES):
            blk = acc[:, g * LANES:(g + 1) * LANES]
            half = ROPE_DIMS // 2
            parts.append(blk * cos + pltpu.roll(blk, LANES - half, 1) * s1 + pltpu.roll(blk, half, 1) * s2)
        acc = jnp.concatenate(parts, axis=1) if len(parts) > 1 else parts[0]
    if scale != 1.0:
        acc = acc * scale
    for o in outs:
        o[...] = acc.astype(o.dtype)


def _proj(x, w, out_dtypes, rope_tabs=None, scale=1.0, tm_pref=512, tn_pref=1024):
    m, k = x.shape
    n = w.shape[1]
    tm, tn = _tile(m, tm_pref), _tile(n, tn_pref)
    in_specs = [pl.BlockSpec((tm, k), lambda i, j: (i, 0)),
                pl.BlockSpec((k, tn), lambda i, j: (0, j))]
    args = [x, w]
    if rope_tabs is not None:
        in_specs += [pl.BlockSpec((tm, LANES), lambda i, j: (i, 0))] * 3
        args += list(rope_tabs)
    outs = pl.pallas_call(
        functools.partial(_proj_kernel, rope=rope_tabs is not None, scale=scale, n_out=len(out_dtypes)),
        grid=(m // tm, n // tn),
        in_specs=in_specs,
        out_specs=[pl.BlockSpec((tm, tn), lambda i, j: (i, j)) for _ in out_dtypes],
        out_shape=[jax.ShapeDtypeStruct((m, n), dt) for dt in out_dtypes],
        compiler_params=_cparams(("parallel", "arbitrary")),
        name="proj",
    )(*args)
    return outs


def _glu_kernel(x_ref, wg_ref, wu_ref, o_ref):
    x = x_ref[...]
    g = _dot(x, wg_ref[...])
    u = _dot(x, wu_ref[...])
    o_ref[...] = (g * jax.nn.sigmoid(g) * u).astype(o_ref.dtype)


def _glu(x, wg, wu):
    m, k = x.shape
    n = wg.shape[1]
    tm, tn = _tile(m, 512), _tile(n, 512)
    return pl.pallas_call(
        _glu_kernel,
        grid=(m // tm, n // tn),
        in_specs=[pl.BlockSpec((tm, k), lambda i, j: (i, 0)),
                  pl.BlockSpec((k, tn), lambda i, j: (0, j)),
                  pl.BlockSpec((k, tn), lambda i, j: (0, j))],
        out_specs=pl.BlockSpec((tm, tn), lambda i, j: (i, j)),
        out_shape=jax.ShapeDtypeStruct((m, n), BF16),
        compiler_params=_cparams(("parallel", "arbitrary")),
        name="shared_glu",
    )(x, wg, wu)


def _merge_kernel(od_ref, os_ref, wd_ref, ws_ref, gd_ref, gs_ref, o_ref):
    bd = _dot(od_ref[...], wd_ref[...])
    bs = _dot(os_ref[...], ws_ref[...])
    gd = jax.nn.sigmoid(gd_ref[...].astype(F32))
    gs = jax.nn.sigmoid(gs_ref[...].astype(F32))
    o_ref[...] = (gd * bd + gs * bs).astype(o_ref.dtype)


def _merge(od, osb, wbd, wbs, gd, gs):
    m, k = od.shape
    n = wbd.shape[1]
    tm, tn = _tile(m, 512), _tile(n, 1024)
    return pl.pallas_call(
        _merge_kernel,
        grid=(m // tm, n // tn),
        in_specs=[pl.BlockSpec((tm, k), lambda i, j: (i, 0)),
                  pl.BlockSpec((tm, k), lambda i, j: (i, 0)),
                  pl.BlockSpec((k, tn), lambda i, j: (0, j)),
                  pl.BlockSpec((k, tn), lambda i, j: (0, j)),
                  pl.BlockSpec((tm, tn), lambda i, j: (i, j)),
                  pl.BlockSpec((tm, tn), lambda i, j: (i, j))],
        out_specs=pl.BlockSpec((tm, tn), lambda i, j: (i, j)),
        out_shape=jax.ShapeDtypeStruct((m, n), BF16),
        compiler_params=_cparams(("parallel", "arbitrary")),
        name="merge",
    )(od, osb, wbd, wbs, gd, gs)


def _lambda(lam_ref, lam_init):
    a = jnp.sum(lam_ref[0:1, :] * lam_ref[1:2, :], axis=-1, keepdims=True)
    b = jnp.sum(lam_ref[2:3, :] * lam_ref[3:4, :], axis=-1, keepdims=True)
    return jnp.exp(a) - jnp.exp(b) + lam_init


def _split_components(q):
    lane = lax.broadcasted_iota(jnp.int32, q.shape, 1)
    zero = jnp.zeros_like(q)
    return jnp.where(lane < DIFF_DK, q, zero), jnp.where(lane >= DIFF_DK, q, zero)


def _diff_finish(o1, l1, o2, l2, lam, g, lam_init):
    o = o1 / l1 - lam * (o2 / l2)
    o = o * lax.rsqrt(jnp.mean(o * o, axis=-1, keepdims=True) + LN_EPS)
    return o * g * (1.0 - lam_init)


def _diff_prompt_kernel(lam_ref, g_ref, q_ref, k_ref, v_ref, o_ref, m_ref, l_ref, acc_ref, *, tq, tk, lam_init):
    i = pl.program_id(1)
    q1, q2 = _split_components(q_ref[...])
    m_ref[...] = jnp.full(m_ref.shape, -jnp.inf, F32)
    l_ref[...] = jnp.zeros(l_ref.shape, F32)
    acc_ref[...] = jnp.zeros(acc_ref.shape, F32)

    def block(start, keep):
        k = k_ref[pl.ds(start, tk), :]
        v = v_ref[pl.ds(start, tk), :]
        for comp, qc in enumerate((q1, q2)):
            s = _dot_t(qc, k)
            if keep is not None:
                s = jnp.where(keep[0], jnp.where(keep[1], s, -1e30), -1e30)
            m_old = m_ref[comp]
            m_new = jnp.maximum(m_old, jnp.max(s, axis=-1, keepdims=True))
            alpha = jnp.exp(m_old - m_new)
            p = jnp.exp(s - m_new)
            l_ref[comp] = alpha * l_ref[comp] + jnp.sum(p, axis=-1, keepdims=True)
            acc_ref[comp] = alpha * acc_ref[comp] + _dot(p.astype(BF16), v)
            m_ref[comp] = m_new

    n_full = (i * tq) // tk

    def body(j, carry):
        block(pl.multiple_of(j * tk, tk), None)
        return carry

    lax.fori_loop(0, n_full, body, 0)
    start = pl.multiple_of(jnp.maximum((i + 1) * tq - tk, 0), tq)
    lo = n_full * tk
    col = start + lax.broadcasted_iota(jnp.int32, (tq, tk), 1)
    row = i * tq + lax.broadcasted_iota(jnp.int32, (tq, 1), 0)
    hi = (row // CHUNK + 1) * CHUNK
    block(start, (col >= lo, col < hi))
    lam = _lambda(lam_ref, lam_init)
    o = _diff_finish(acc_ref[0], l_ref[0], acc_ref[1], l_ref[1], lam, g_ref[...], lam_init)
    o_ref[...] = o.astype(o_ref.dtype)


def _diff_prompt(q, k, v, lam4, g, lam_init):
    t, w = q.shape
    heads = w // HEAD_W
    tq = _tile(t, 256)
    tk = _tile(t, 1024)
    assert tq % CHUNK == 0 and tk % tq == 0
    return pl.pallas_call(
        functools.partial(_diff_prompt_kernel, tq=tq, tk=tk, lam_init=lam_init),
        grid=(heads, t // tq),
        in_specs=[pl.BlockSpec((4, DIFF_DK), lambda h, i: (0, 0)),
                  pl.BlockSpec((1, HEAD_W), lambda h, i: (0, 0)),
                  pl.BlockSpec((tq, HEAD_W), lambda h, i: (i, h)),
                  pl.BlockSpec((t, HEAD_W), lambda h, i: (0, h)),
                  pl.BlockSpec((t, HEAD_W), lambda h, i: (0, h))],
        out_specs=pl.BlockSpec((tq, HEAD_W), lambda h, i: (i, h)),
        out_shape=jax.ShapeDtypeStruct((t, w), BF16),
        scratch_shapes=[pltpu.VMEM((2, tq, 1), F32), pltpu.VMEM((2, tq, 1), F32),
                        pltpu.VMEM((2, tq, HEAD_W), F32)],
        compiler_params=_cparams(("parallel", "arbitrary")),
        name="diff_prompt",
    )(lam4, g, q, k, v)


def _online_softmax_update(idx, s, v, m_ref, l_ref, acc_ref):
    m_old = m_ref[idx]
    m_new = jnp.maximum(m_old, jnp.max(s, axis=-1, keepdims=True))
    alpha = jnp.exp(m_old - m_new)
    p = jnp.exp(s - m_new)
    l_ref[idx] = alpha * l_ref[idx] + jnp.sum(p, axis=-1, keepdims=True)
    acc_ref[idx] = alpha * acc_ref[idx] + _dot(p.astype(BF16), v)
    m_ref[idx] = m_new


def _head_rows(cache_ref, head, start, size, heads):
    return cache_ref[pl.ds(start * heads + head, size, stride=heads), :]


def _diff_sample_kernel(lam_ref, g_ref, q_ref, kn_ref, vn_ref, kc_ref, vc_ref, o_ref, m_ref, l_ref, acc_ref, *,
                        past_len, heads, pc, lam_init):
    c = pl.program_id(1)
    tn = q_ref.shape[1]

    @pl.when(c == 0)
    def _():
        m_ref[...] = jnp.full(m_ref.shape, -jnp.inf, F32)
        l_ref[...] = jnp.zeros(l_ref.shape, F32)
        acc_ref[...] = jnp.zeros(acc_ref.shape, F32)
        r = (past_len + lax.broadcasted_iota(jnp.int32, (tn, tn), 0)) // CHUNK
        col = (past_len + lax.broadcasted_iota(jnp.int32, (tn, tn), 1)) // CHUNK
        keep = col <= r

        def new_rows(h, carry):
            kn = kn_ref[h].astype(BF16)
            vn = vn_ref[h].astype(BF16)
            for comp, qc in enumerate(_split_components(q_ref[h])):
                s = jnp.where(keep, _dot_t(qc, kn), -1e30)
                _online_softmax_update(2 * h + comp, s, vn, m_ref, l_ref, acc_ref)
            return carry

        lax.fori_loop(0, heads, new_rows, 0)

    def past_rows(h, carry):
        kp = _head_rows(kc_ref, h, 0, pc, heads).astype(BF16)
        vp = _head_rows(vc_ref, h, 0, pc, heads).astype(BF16)
        for comp, qc in enumerate(_split_components(q_ref[h])):
            _online_softmax_update(2 * h + comp, _dot_t(qc, kp), vp, m_ref, l_ref, acc_ref)
        return carry

    lax.fori_loop(0, heads, past_rows, 0, unroll=2)

    @pl.when(c == pl.num_programs(1) - 1)
    def _():
        lam = _lambda(lam_ref, lam_init)

        def finish(h, carry):
            o = _diff_finish(acc_ref[2 * h], l_ref[2 * h], acc_ref[2 * h + 1], l_ref[2 * h + 1], lam, g_ref[...],
                             lam_init)
            o_ref[h] = o.astype(o_ref.dtype)
            return carry

        lax.fori_loop(0, heads, finish, 0)


def _head_major(x, b, s):
    return x.reshape(b, s, -1, HEAD_W).transpose(0, 2, 1, 3)


def _token_major(x):
    b, h, s, w = x.shape
    return x.transpose(0, 2, 1, 3).reshape(b * s, h * w)


def _sample_specs(seq_len, heads, pc, chunk_index):
    row = pl.BlockSpec((None, heads, seq_len, HEAD_W), lambda bi, c: (bi, 0, 0, 0))
    cache = pl.BlockSpec((None, pc * heads, HEAD_W), lambda bi, c: (bi, chunk_index(c), 0))
    return row, cache


def _diff_sample(q, kn, vn, kc, vc, lam4, g, lam_init, seq_len):
    t, w = q.shape
    heads = w // HEAD_W
    b = kc.shape[0]
    past_len = kc.shape[1] // heads
    assert past_len % CHUNK == 0
    pc = _tile(past_len, SAMPLE_PAST_CHUNK)
    row, cache = _sample_specs(seq_len, heads, pc, lambda c: c)
    out = pl.pallas_call(
        functools.partial(_diff_sample_kernel, past_len=past_len, heads=heads, pc=pc, lam_init=lam_init),
        grid=(b, past_len // pc),
        in_specs=[pl.BlockSpec((4, DIFF_DK), lambda bi, c: (0, 0)),
                  pl.BlockSpec((1, HEAD_W), lambda bi, c: (0, 0)),
                  row, row, row, cache, cache],
        out_specs=row,
        out_shape=jax.ShapeDtypeStruct((b, heads, seq_len, HEAD_W), BF16),
        scratch_shapes=[pltpu.VMEM((2 * heads, seq_len, 1), F32), pltpu.VMEM((2 * heads, seq_len, 1), F32),
                        pltpu.VMEM((2 * heads, seq_len, HEAD_W), F32)],
        compiler_params=_cparams(("parallel", "arbitrary")),
        name="diff_sample",
    )(lam4, g, _head_major(q, b, seq_len), _head_major(kn, b, seq_len), _head_major(vn, b, seq_len), kc, vc)
    return _token_major(out)


def _sb_block(q, k, v, carry_ref, acc_ref, keep):
    tk = k.shape[0]
    z = _dot_t(q, k)
    soft = jnp.log(1.0 + jnp.exp(-jnp.abs(z)))
    log_sig = jnp.minimum(z, 0.0) - soft
    log_keep = jnp.minimum(-z, 0.0) - soft
    if keep is not None:
        log_keep = jnp.where(keep, log_keep, 0.0)
    upper = (lax.broadcasted_iota(jnp.int32, (tk, tk), 0) > lax.broadcasted_iota(jnp.int32, (tk, tk), 1))
    upper = jnp.where(upper, 1.0, 0.0).astype(BF16)
    hi = log_keep.astype(BF16)
    lo = (log_keep - hi.astype(F32)).astype(BF16)
    later = _dot(hi, upper) + _dot(lo, upper) + carry_ref[...]
    w = jnp.exp(log_sig + later)
    if keep is not None:
        w = jnp.where(keep, w, 0.0)
    acc_ref[...] += _dot(w.astype(BF16), v)
    carry_ref[...] += jnp.sum(log_keep, axis=-1, keepdims=True)


def _sb_prompt_kernel(q_ref, k_ref, v_ref, o_ref, carry_ref, acc_ref, *, tq):
    i = pl.program_id(1)
    q = q_ref[...]
    carry_ref[...] = jnp.zeros(carry_ref.shape, F32)
    acc_ref[...] = jnp.zeros(acc_ref.shape, F32)

    def load(j):
        start = pl.multiple_of(j * tq, tq)
        return k_ref[pl.ds(start, tq), :], v_ref[pl.ds(start, tq), :]

    strict = (lax.broadcasted_iota(jnp.int32, (tq, tq), 1) < lax.broadcasted_iota(jnp.int32, (tq, tq), 0))
    k, v = load(i)
    _sb_block(q, k, v, carry_ref, acc_ref, strict)

    def cond(state):
        j, alive = state
        return jnp.logical_and(j >= 0, alive > SB_DEAD_LOG)

    def body(state):
        j, _ = state
        kj, vj = load(j)
        _sb_block(q, kj, vj, carry_ref, acc_ref, None)
        return j - 1, jnp.max(carry_ref[...])

    lax.while_loop(cond, body, (i - 1, jnp.max(carry_ref[...])))
    o_ref[...] = acc_ref[...].astype(o_ref.dtype)


def _sb_prompt(q, k, v):
    t, w = q.shape
    heads = w // HEAD_W
    tq = _tile(t, 256)
    return pl.pallas_call(
        functools.partial(_sb_prompt_kernel, tq=tq),
        grid=(heads, t // tq),
        in_specs=[pl.BlockSpec((tq, HEAD_W), lambda h, i: (i, h)),
                  pl.BlockSpec((t, HEAD_W), lambda h, i: (0, h)),
                  pl.BlockSpec((t, HEAD_W), lambda h, i: (0, h))],
        out_specs=pl.BlockSpec((tq, HEAD_W), lambda h, i: (i, h)),
        out_shape=jax.ShapeDtypeStruct((t, w), BF16),
        scratch_shapes=[pltpu.VMEM((tq, 1), F32), pltpu.VMEM((tq, HEAD_W), F32)],
        compiler_params=_cparams(("parallel", "arbitrary")),
        name="sb_prompt",
    )(q, k, v)


def _sb_sample_kernel(q_ref, kn_ref, vn_ref, kc_ref, vc_ref, o_ref, carry_ref, acc_ref, *, heads, pc, tk):
    c = pl.program_id(1)
    tn = q_ref.shape[1]

    @pl.when(c == 0)
    def _():
        carry_ref[...] = jnp.zeros(carry_ref.shape, F32)
        acc_ref[...] = jnp.zeros(acc_ref.shape, F32)
        strict = (lax.broadcasted_iota(jnp.int32, (tn, tn), 1) < lax.broadcasted_iota(jnp.int32, (tn, tn), 0))

        def new_rows(h, carry):
            _sb_block(q_ref[h], kn_ref[h].astype(BF16), vn_ref[h].astype(BF16), carry_ref.at[h], acc_ref.at[h], strict)
            return carry

        lax.fori_loop(0, heads, new_rows, 0)

    for j in reversed(range(pc // tk)):
        @pl.when(jnp.max(carry_ref[...]) > SB_DEAD_LOG)
        def _():
            def past_rows(h, carry):
                kj = _head_rows(kc_ref, h, j * tk, tk, heads).astype(BF16)
                vj = _head_rows(vc_ref, h, j * tk, tk, heads).astype(BF16)
                _sb_block(q_ref[h], kj, vj, carry_ref.at[h], acc_ref.at[h], None)
                return carry

            lax.fori_loop(0, heads, past_rows, 0)

    @pl.when(c == pl.num_programs(1) - 1)
    def _():
        o_ref[...] = acc_ref[...].astype(o_ref.dtype)


def _sb_sample(q, kn, vn, kc, vc, seq_len):
    t, w = q.shape
    heads = w // HEAD_W
    b = kc.shape[0]
    past_len = kc.shape[1] // heads
    pc = _tile(past_len, SAMPLE_PAST_CHUNK)
    n_c = past_len // pc
    tk = _tile(pc, 256)
    row, cache = _sample_specs(seq_len, heads, pc, lambda c: n_c - 1 - c)
    out = pl.pallas_call(
        functools.partial(_sb_sample_kernel, heads=heads, pc=pc, tk=tk),
        grid=(b, n_c),
        in_specs=[row, row, row, cache, cache],
        out_specs=row,
        out_shape=jax.ShapeDtypeStruct((b, heads, seq_len, HEAD_W), BF16),
        scratch_shapes=[pltpu.VMEM((heads, seq_len, 1), F32), pltpu.VMEM((heads, seq_len, HEAD_W), F32)],
        compiler_params=_cparams(("parallel", "arbitrary")),
        name="sb_sample",
    )(_head_major(q, b, seq_len), _head_major(kn, b, seq_len), _head_major(vn, b, seq_len), kc, vc)
    return _token_major(out)


def _pack_halves(x):
    n = x.shape[1] // 2
    bits = pltpu.bitcast(x.astype(BF16).astype(F32), jnp.uint32)
    return bits[:, n:] | (bits[:, :n] >> 16)


def _unpack_halves(u):
    lo = pltpu.bitcast(u << 16, F32)
    hi = pltpu.bitcast(u & jnp.uint32(0xFFFF0000), F32)
    return lo, hi


def _resid_norm_kernel(x_ref, a_ref, gate_ref, g_ref, b_ref, shift_ref, scale_ref, x1_ref, h_ref, hp_ref, *, alpha):
    x1 = _ln(alpha * x_ref[...] + gate_ref[...] * a_ref[...]) * g_ref[...] + b_ref[...]
    x1_ref[...] = x1
    h = _ln(x1) * (1.0 + scale_ref[...]) + shift_ref[...]
    h_ref[...] = h.astype(h_ref.dtype)
    hp_ref[...] = _pack_halves(h)


def _resid_norm(x, a, mod3, row0, seq_len, g, b, alpha):
    t, d = x.shape
    tm = _tile(seq_len, 256)
    tps = seq_len // tm
    tok = pl.BlockSpec((tm, d), lambda i: (i, 0))
    vec = pl.BlockSpec((1, d), lambda i: (0, 0))
    return pl.pallas_call(
        functools.partial(_resid_norm_kernel, alpha=alpha),
        grid=(t // tm,),
        in_specs=[tok, tok, _mod_spec(d, 2, tps, row0), vec, vec,
                  _mod_spec(d, 3, tps, row0), _mod_spec(d, 4, tps, row0)],
        out_specs=[tok, tok, pl.BlockSpec((tm, d // 2), lambda i: (i, 0))],
        out_shape=[jax.ShapeDtypeStruct((t, d), F32), jax.ShapeDtypeStruct((t, d), BF16),
                   jax.ShapeDtypeStruct((t, d // 2), jnp.uint32)],
        compiler_params=_cparams(("parallel",)),
        name="resid_norm",
    )(x, a, mod3, g, b, mod3, mod3)


def _router_kernel(h_ref, whi_ref, wlo_ref, bias_ref, gate_ref, idx_ref, rank_ref, cnt_ref, run_ref):
    @pl.when(pl.program_id(0) == 0)
    def _():
        run_ref[...] = jnp.zeros(run_ref.shape, F32)

    h = h_ref[...]
    logits = _dot(h, whi_ref[...]) + _dot(h, wlo_ref[...])
    scores = jax.nn.sigmoid(logits)
    tm, e = scores.shape
    lane = lax.broadcasted_iota(jnp.int32, (tm, e), 1)
    sel = scores + bias_ref[...]
    picked = jnp.zeros((tm, e), F32)
    idx = jnp.zeros((tm, e), jnp.int32)
    firsts = []
    for k in range(TOP_K):
        best = jnp.max(sel, axis=-1, keepdims=True)
        first = jnp.min(jnp.where(sel == best, lane, e), axis=-1, keepdims=True)
        hit = lane == first
        picked = jnp.where(hit, 1.0, picked)
        sel = jnp.where(hit, -jnp.inf, sel)
        idx = jnp.where(lane == k, first, idx)
        firsts.append(first)
    gate = scores * picked
    gate = gate / jnp.sum(gate, axis=-1, keepdims=True) * ROUTE_SCALE
    lower = (lax.broadcasted_iota(jnp.int32, (tm, tm), 1) < lax.broadcasted_iota(jnp.int32, (tm, tm), 0))
    within = _dot(jnp.where(lower, 1.0, 0.0).astype(BF16), picked.astype(BF16)) + run_ref[...]
    gate8 = jnp.zeros((tm, e), F32)
    rank8 = jnp.zeros((tm, e), F32)
    for k, first in enumerate(firsts):
        hit = lane == first
        gk = jnp.sum(jnp.where(hit, gate, 0.0), axis=-1, keepdims=True)
        rk = jnp.sum(jnp.where(hit, within, 0.0), axis=-1, keepdims=True)
        gate8 = jnp.where(lane == k, gk, gate8)
        rank8 = jnp.where(lane == k, rk, rank8)
    gate_ref[...] = gate8
    idx_ref[...] = idx
    rank_ref[...] = rank8.astype(jnp.int32)
    run_ref[...] += jnp.sum(picked, axis=0, keepdims=True)
    cnt_ref[...] = run_ref[...].astype(jnp.int32)


def _router(h, w_router, router_bias):
    t, d = h.shape
    e = w_router.shape[1]
    assert e == LANES
    tm = _tile(t, 512)
    whi = w_router.astype(BF16)
    wlo = (w_router - whi.astype(F32)).astype(BF16)
    tok = pl.BlockSpec((tm, e), lambda i: (i, 0))
    wspec = pl.BlockSpec((d, e), lambda i: (0, 0))
    one = pl.BlockSpec((1, e), lambda i: (0, 0))
    return pl.pallas_call(
        _router_kernel,
        grid=(t // tm,),
        in_specs=[pl.BlockSpec((tm, d), lambda i: (i, 0)), wspec, wspec, one],
        out_specs=[tok, tok, tok, one],
        out_shape=[jax.ShapeDtypeStruct((t, e), F32), jax.ShapeDtypeStruct((t, e), jnp.int32),
                   jax.ShapeDtypeStruct((t, e), jnp.int32), jax.ShapeDtypeStruct((1, e), jnp.int32)],
        scratch_shapes=[pltpu.VMEM((1, e), F32)],
        compiler_params=_cparams(("arbitrary",)),
        name="router",
    )(h, whi, wlo, router_bias.reshape(1, e))


def _row_copy(src_hbm, tok, buf, slot, r, sem):
    return pltpu.make_async_copy(src_hbm.at[pl.ds(tok, 1)], buf.at[slot, pl.ds(r, 1)], sem.at[slot])


def _expert_up_kernel(bexp_ref, nused_ref, tok_ref, tok_next_ref, h_hbm, wg_ref, wu_ref, o_ref, buf, sem, *,
                      blk, n_blocks):
    b = pl.program_id(0)
    n_used = nused_ref[0]
    slot = lax.rem(b, 2)

    def start_rows(toks, into):
        for r in range(blk):
            _row_copy(h_hbm, toks[0, r], buf, into, r, sem).start()

    def wait_rows(into):
        for r in range(blk):
            _row_copy(h_hbm, 0, buf, into, r, sem).wait()

    @pl.when(b == 0)
    def _():
        def body(r, carry):
            _row_copy(h_hbm, tok_ref[0, r], buf, slot, r, sem).start()
            return carry
        lax.fori_loop(0, blk, body, 0)

    @pl.when(b < n_used)
    def _():
        wait_rows(slot)
        start_rows(tok_next_ref, 1 - slot)
        x_lo, x_hi = (v.astype(BF16) for v in _unpack_halves(buf[slot]))
        n = x_lo.shape[1]
        g = _dot(x_lo, wg_ref[:n, :].astype(BF16)) + _dot(x_hi, wg_ref[n:, :].astype(BF16))
        u = _dot(x_lo, wu_ref[:n, :].astype(BF16)) + _dot(x_hi, wu_ref[n:, :].astype(BF16))
        o_ref[...] = (g * jax.nn.sigmoid(g) * u).astype(o_ref.dtype)

        @pl.when(b == n_blocks - 1)
        def _():
            wait_rows(1 - slot)

    @pl.when(b == n_used)
    def _():
        wait_rows(slot)

    @pl.when(b >= n_used)
    def _():
        o_ref[...] = jnp.zeros(o_ref.shape, o_ref.dtype)


def _expert_up(hp, wg, wu, slot_tok, block_expert, n_used, blk):
    e, d, hid = wg.shape
    assert hp.shape[1] * 2 == d
    n_blocks = block_expert.shape[0]
    wspec = pl.BlockSpec((None, d, hid), lambda b, be, nu: (be[b], 0, 0))
    tok3 = slot_tok.reshape(n_blocks, 1, blk)
    grid_spec = pltpu.PrefetchScalarGridSpec(
        num_scalar_prefetch=2,
        grid=(n_blocks,),
        in_specs=[pl.BlockSpec((None, 1, blk), lambda b, be, nu: (b, 0, 0), memory_space=pltpu.SMEM),
                  pl.BlockSpec((None, 1, blk), lambda b, be, nu: (jnp.minimum(b + 1, n_blocks - 1), 0, 0),
                               memory_space=pltpu.SMEM),
                  pl.BlockSpec(memory_space=pl.ANY), wspec, wspec],
        out_specs=pl.BlockSpec((blk, hid), lambda b, be, nu: (b, 0)),
        scratch_shapes=[pltpu.VMEM((2, blk, d // 2), jnp.uint32), pltpu.SemaphoreType.DMA((2,))],
    )
    return pl.pallas_call(
        functools.partial(_expert_up_kernel, blk=blk, n_blocks=n_blocks),
        grid_spec=grid_spec,
        out_shape=jax.ShapeDtypeStruct((n_blocks * blk, hid), BF16),
        compiler_params=_cparams(("arbitrary",)),
        name="expert_up",
    )(block_expert, n_used, tok3, tok3, hp, wg, wu)


def _expert_down_kernel(bexp_ref, nused_ref, a_ref, wd_ref, o_ref):
    @pl.when(pl.program_id(0) < nused_ref[0])
    def _():
        o_ref[...] = _pack_halves(_dot(a_ref[...], wd_ref[...].astype(BF16)))

    @pl.when(pl.program_id(0) >= nused_ref[0])
    def _():
        o_ref[...] = jnp.zeros(o_ref.shape, o_ref.dtype)


def _expert_down(act, wd, block_expert, n_used, blk):
    e, hid, d = wd.shape
    n_blocks = block_expert.shape[0]
    grid_spec = pltpu.PrefetchScalarGridSpec(
        num_scalar_prefetch=2,
        grid=(n_blocks,),
        in_specs=[pl.BlockSpec((blk, hid), lambda b, be, nu: (b, 0)),
                  pl.BlockSpec((None, hid, d), lambda b, be, nu: (be[b], 0, 0))],
        out_specs=pl.BlockSpec((blk, d // 2), lambda b, be, nu: (b, 0)),
    )
    return pl.pallas_call(
        _expert_down_kernel,
        grid_spec=grid_spec,
        out_shape=jax.ShapeDtypeStruct((n_blocks * blk, d // 2), jnp.uint32),
        compiler_params=_cparams(("arbitrary",)),
        name="expert_down",
    )(block_expert, n_used, act, wd)


def _combine_kernel(dest_ref, dest_next_ref, y_hbm, gate_ref, sh_ref, x_ref, gf_ref, g_ref, b_ref, o_ref, buf, sem, *,
                    tt, n_steps, alpha):
    i = pl.program_id(0)
    slot = lax.rem(i, 2)

    def copy(src_row, into, r, k):
        return pltpu.make_async_copy(y_hbm.at[pl.ds(src_row, 1)], buf.at[into, k, pl.ds(r, 1)], sem.at[into])

    def start_rows(dref, into):
        for r in range(tt):
            for k in range(TOP_K):
                copy(dref[0, r * TOP_K + k], into, r, k).start()

    def wait_rows(into):
        for r in range(tt):
            for k in range(TOP_K):
                copy(0, into, r, k).wait()

    @pl.when(i == 0)
    def _():
        def body(r, carry):
            for k in range(TOP_K):
                copy(dest_ref[0, r * TOP_K + k], slot, r, k).start()
            return carry
        lax.fori_loop(0, tt, body, 0)

    wait_rows(slot)
    start_rows(dest_next_ref, 1 - slot)
    gate = gate_ref[...]
    n = buf.shape[-1]
    f_lo, f_hi = sh_ref[:, :n], sh_ref[:, n:]
    for k in range(TOP_K):
        y_lo, y_hi = _unpack_halves(buf[slot, k])
        f_lo = f_lo + y_lo * gate[:, k:k + 1]
        f_hi = f_hi + y_hi * gate[:, k:k + 1]
    f = jnp.concatenate([f_lo, f_hi], axis=1)
    o_ref[...] = _ln(alpha * x_ref[...] + gf_ref[...] * f) * g_ref[...] + b_ref[...]

    @pl.when(i == n_steps - 1)
    def _():
        wait_rows(1 - slot)


def _combine(y_slots, dest8, gate8, shared, x1, mod3, row0, seq_len, tok0, n_tok, g, b, alpha):
    d = x1.shape[1]
    tt = _tile(seq_len, 64)
    assert tok0 % tt == 0
    tile0 = tok0 // tt
    tps = seq_len // tt
    n_steps = n_tok // tt
    tok = pl.BlockSpec((tt, d), lambda i: (tile0 + i, 0))
    vec = pl.BlockSpec((1, d), lambda i: (0, 0))
    dest3 = dest8.reshape(-1, 1, tt * TOP_K)
    return pl.pallas_call(
        functools.partial(_combine_kernel, tt=tt, n_steps=n_steps, alpha=alpha),
        grid=(n_steps,),
        in_specs=[pl.BlockSpec((None, 1, tt * TOP_K), lambda i: (tile0 + i, 0, 0), memory_space=pltpu.SMEM),
                  pl.BlockSpec((None, 1, tt * TOP_K), lambda i: (tile0 + jnp.minimum(i + 1, n_steps - 1), 0, 0),
                               memory_space=pltpu.SMEM),
                  pl.BlockSpec(memory_space=pl.ANY),
                  pl.BlockSpec((tt, LANES), lambda i: (tile0 + i, 0)),
                  tok, tok,
                  pl.BlockSpec((None, 1, d), lambda i: ((row0 + i // tps) * 6 + 5, 0, 0)),
                  vec, vec],
        out_specs=pl.BlockSpec((tt, d), lambda i: (i, 0)),
        out_shape=jax.ShapeDtypeStruct((n_tok, d), F32),
        scratch_shapes=[pltpu.VMEM((2, TOP_K, tt, d // 2), jnp.uint32), pltpu.SemaphoreType.DMA((2,))],
        compiler_params=_cparams(("arbitrary",)),
        name="combine",
    )(dest3, dest3, y_slots, gate8, shared, x1, mod3, g, b)


def _rope_tables(pos):
    half = ROPE_DIMS // 2
    inv_freq = ROPE_THETA ** (-jnp.arange(half, dtype=F32) / half)
    ang = pos.astype(F32)[:, None] * inv_freq[None, :]
    cos, sin = jnp.cos(ang), jnp.sin(ang)
    n = pos.shape[0]
    ones = jnp.ones((n, DIFF_DK - ROPE_DIMS), F32)
    zeros = jnp.zeros((n, DIFF_DK - ROPE_DIMS), F32)
    zh = jnp.zeros((n, half), F32)
    c64 = jnp.concatenate([cos, cos, ones], axis=1)
    s1 = jnp.concatenate([-sin, zh, zeros], axis=1)
    s2 = jnp.concatenate([zh, sin, zeros], axis=1)
    rep = LANES // DIFF_DK
    return tuple(jnp.tile(a, (1, rep)) for a in (c64, s1, s2))


def _mixer(x, mod3, row0, seq_len, past, wts, lam4, g_norm, lam_init):
    (w_qd, w_kd, w_vd, w_qs, w_ks, w_vs, w_gd, w_gs, w_bd, w_bs, w_out) = wts
    t = x.shape[0]
    past_len = 0 if past is None else past[0].shape[1] // (w_kd.shape[1] // HEAD_W)
    pos = past_len + jnp.arange(seq_len, dtype=jnp.int32)
    tabs = _rope_tables(jnp.tile(pos, t // seq_len))
    h = _ln_mod(x, mod3, row0, seq_len, 0, 1)
    (qd,) = _proj(h, w_qd, (BF16,), rope_tabs=tabs, scale=DIFF_DK ** -0.5)
    (qs,) = _proj(h, w_qs, (BF16,), scale=HEAD_W ** -0.5)
    (gd,) = _proj(h, w_gd, (BF16,))
    (gs,) = _proj(h, w_gs, (BF16,))
    if past is None:
        kd, kd_b = _proj(h, w_kd, (F32, BF16), rope_tabs=tabs)
        vd, vd_b = _proj(h, w_vd, (F32, BF16))
        ks, ks_b = _proj(h, w_ks, (F32, BF16))
        vs, vs_b = _proj(h, w_vs, (F32, BF16))
        od = _diff_prompt(qd, kd_b, vd_b, lam4, g_norm, lam_init)
        osb = _sb_prompt(qs, ks_b, vs_b)
    else:
        (kd,) = _proj(h, w_kd, (F32,), rope_tabs=tabs)
        (vd,) = _proj(h, w_vd, (F32,))
        (ks,) = _proj(h, w_ks, (F32,))
        (vs,) = _proj(h, w_vs, (F32,))
        od = _diff_sample(qd, kd, vd, past[0], past[1], lam4, g_norm, lam_init, seq_len)
        osb = _sb_sample(qs, ks, vs, past[2], past[3], seq_len)
    merged = _merge(od, osb, w_bd, w_bs, gd, gs)
    (a,) = _proj(merged, w_out, (F32,))
    return a, (kd, vd, ks, vs)


def _moe_plan(idx8, rank8, counts, n_tok, blk):
    e = counts.shape[0]
    padded = (counts + blk - 1) // blk * blk
    pad_ends = jnp.cumsum(padded)
    pad_starts = pad_ends - padded
    hit = idx8[..., None] == jnp.arange(e, dtype=jnp.int32)
    dest8 = (jnp.sum(jnp.where(hit, pad_starts, 0), axis=-1) + rank8).astype(jnp.int32)
    n_blocks = -(-n_tok * TOP_K // blk) + e
    tok_ids = jnp.repeat(jnp.arange(n_tok, dtype=jnp.int32), TOP_K)
    slot_tok = jnp.zeros((n_blocks * blk,), jnp.int32).at[dest8.reshape(-1)].set(tok_ids)
    block_pos = jnp.arange(n_blocks, dtype=jnp.int32) * blk
    block_expert = jnp.minimum(jnp.searchsorted(pad_ends, block_pos, side='right'), e - 1).astype(jnp.int32)
    n_used = (pad_ends[-1] // blk).astype(jnp.int32).reshape(1)
    return dest8, slot_tok, block_expert, n_used


def kernel(x_prompt, x_sample, c_prompt, c_sample, cache_diff_k, cache_diff_v, cache_sb_k, cache_sb_v, w_ada, b_ada, w_in, w_br_diff, w_br_sb, w_out, lam_q1, lam_k1, lam_q2, lam_k2, diff_norm_g, ln1_g, ln1_b, w_router, router_bias, w_e_gate, w_e_up, w_e_down, w_sh_gate, w_sh_up, w_sh_down, ln2_g, ln2_b):
    depth = w_ada.shape[0]
    assert depth == 1
    bp, sp, d = x_prompt.shape
    bs, ss, _ = x_sample.shape
    assert bp == 1
    diff_w = w_br_diff.shape[1]
    sb_w = w_br_sb.shape[1]
    heads_d, heads_s = diff_w // HEAD_W, sb_w // HEAD_W
    past_len = cache_diff_k.shape[2]
    alpha = (2.0 * depth) ** 0.25
    lam_init = 0.8 - 0.6 * float(np.exp(-0.3 * 0))
    blk = 256

    widths = [diff_w, diff_w, diff_w, sb_w, sb_w, sb_w, d, d]
    offs = np.concatenate([[0], np.cumsum(widths)])
    w_in0 = w_in[0]
    w_qd, w_kd, w_vd, w_qs, w_ks, w_vs, w_gd, w_gs = (
        w_in0[:, offs[i]:offs[i + 1]].astype(BF16) for i in range(8))
    wts = (w_qd, w_kd, w_vd, w_qs, w_ks, w_vs, w_gd, w_gs,
           w_br_diff[0].astype(BF16), w_br_sb[0].astype(BF16), w_out[0].astype(BF16))
    lam4 = jnp.stack([lam_q1[0], lam_k1[0], lam_q2[0], lam_k2[0]]).astype(F32)
    g_norm = diff_norm_g[0].reshape(1, HEAD_W).astype(F32)

    c_all = jnp.concatenate([c_prompt, c_sample], axis=0)
    n_seq = c_all.shape[0]
    c_pad = jnp.pad(c_all, ((0, -n_seq % 8), (0, 0)))
    mod = _ada(c_pad, w_ada[0], b_ada[0])
    mod3 = mod.reshape(c_pad.shape[0] * 6, 1, d)

    xp = x_prompt.reshape(bp * sp, d)
    xs = x_sample.reshape(bs * ss, d)
    past = (cache_diff_k[0].reshape(bs, past_len * heads_d, HEAD_W), cache_diff_v[0].reshape(bs, past_len * heads_d, HEAD_W),
            cache_sb_k[0].reshape(bs, past_len * heads_s, HEAD_W), cache_sb_v[0].reshape(bs, past_len * heads_s, HEAD_W))

    a_p, rows_p = _mixer(xp, mod3, 0, sp, None, wts, lam4, g_norm, lam_init)
    a_s, rows_s = _mixer(xs, mod3, bp, ss, past, wts, lam4, g_norm, lam_init)
    g1, b1 = ln1_g[0].reshape(1, d), ln1_b[0].reshape(1, d)
    x1_p, h2_p, h2pk_p = _resid_norm(xp, a_p, mod3, 0, sp, g1, b1, alpha)
    x1_s, h2_s, h2pk_s = _resid_norm(xs, a_s, mod3, bp, ss, g1, b1, alpha)

    n_p, n_s = bp * sp, bs * ss
    n_tok = n_p + n_s
    h2 = jnp.concatenate([h2_p, h2_s], axis=0)
    h2pk = jnp.concatenate([h2pk_p, h2pk_s], axis=0)
    x1 = jnp.concatenate([x1_p, x1_s], axis=0)
    gate8, idx8, rank8, counts = _router(h2, w_router[0], router_bias[0])
    dest8, slot_tok, block_expert, n_used = _moe_plan(idx8[:, :TOP_K], rank8[:, :TOP_K], counts[0], n_tok, blk)
    act = _expert_up(h2pk, w_e_gate[0], w_e_up[0], slot_tok, block_expert, n_used, blk)
    y_slots = _expert_down(act, w_e_down[0], block_expert, n_used, blk)
    sh_act = _glu(h2, w_sh_gate[0].astype(BF16), w_sh_up[0].astype(BF16))
    (shared,) = _proj(sh_act, w_sh_down[0].astype(BF16), (F32,))
    g2, b2 = ln2_g[0].reshape(1, d), ln2_b[0].reshape(1, d)
    y_p = _combine(y_slots, dest8, gate8, shared, x1, mod3, 0, sp, 0, n_p, g2, b2, alpha)
    y_s = _combine(y_slots, dest8, gate8, shared, x1, mod3, bp, ss, n_p, n_s, g2, b2, alpha)

    def rows(r, b, s):
        kd, vd, ks, vs = r
        return (kd.reshape(1, b, s, heads_d, 2, DIFF_DK), vd.reshape(1, b, s, heads_d, HEAD_W),
                ks.reshape(1, b, s, heads_s, HEAD_W), vs.reshape(1, b, s, heads_s, HEAD_W))

    return (y_p.reshape(bp, sp, d), y_s.reshape(bs, ss, d)) + rows(rows_p, bp, sp) + rows(rows_s, bs, ss)
```

```python
import functools

import jax
import jax.numpy as jnp
import numpy as np
from jax import lax
from jax.experimental import pallas as pl
from jax.experimental.pallas import tpu as pltpu

F32 = jnp.float32
BF16 = jnp.bfloat16

CHUNK = 64
DIFF_DK = 64
HEAD_W = 128
ROPE_THETA = 500000.0
ROPE_DIMS = DIFF_DK // 4
TOP_K = 8
ROUTE_SCALE = 2.5
LN_EPS = 1e-5
LANES = 128
VMEM_LIMIT = 52 * 1024 * 1024
SB_DEAD_LOG = -104.0
SAMPLE_PAST_CHUNK = 1024


def _cparams(sem):
    return pltpu.CompilerParams(dimension_semantics=sem, vmem_limit_bytes=VMEM_LIMIT)


def _tile(n, pref):
    if n <= pref:
        return n
    t = pref - pref % 64
    while n % t:
        t -= 64
    assert t > 0, (n, pref)
    return t


def _ln(x):
    xc = x - jnp.mean(x, axis=-1, keepdims=True)
    return xc * lax.rsqrt(jnp.mean(xc * xc, axis=-1, keepdims=True) + LN_EPS)


def _dot(a, b):
    return jnp.dot(a, b, preferred_element_type=F32)


def _dot_t(a, b):
    return lax.dot_general(a, b, (((1,), (1,)), ((), ())), preferred_element_type=F32)


def _ada_kernel(c_ref, w_ref, b_ref, o_ref):
    c = c_ref[...]
    s = (c * jax.nn.sigmoid(c)).astype(BF16)
    o_ref[...] = _dot(s, w_ref[...].astype(BF16)) + b_ref[...]


def _ada(c, w_ada, b_ada):
    m, d = c.shape
    n = w_ada.shape[1]
    tn = _tile(n, 512)
    return pl.pallas_call(
        _ada_kernel,
        grid=(n // tn,),
        in_specs=[pl.BlockSpec((m, d), lambda j: (0, 0)),
                  pl.BlockSpec((d, tn), lambda j: (0, j)),
                  pl.BlockSpec((1, tn), lambda j: (0, j))],
        out_specs=pl.BlockSpec((m, tn), lambda j: (0, j)),
        out_shape=jax.ShapeDtypeStruct((m, n), F32),
        compiler_params=_cparams(("arbitrary",)),
        name="ada",
    )(c, w_ada, b_ada.reshape(1, n))


def _ln_mod_kernel(x_ref, shift_ref, scale_ref, h_ref):
    h_ref[...] = (_ln(x_ref[...]) * (1.0 + scale_ref[...]) + shift_ref[...]).astype(h_ref.dtype)


def _mod_spec(d, comp, tiles_per_seq, row0=0):
    return pl.BlockSpec((None, 1, d), lambda i: ((row0 + i // tiles_per_seq) * 6 + comp, 0, 0))


def _ln_mod(x, mod3, row0, seq_len, shift_comp, scale_comp):
    t, d = x.shape
    tm = _tile(seq_len, 256)
    tps = seq_len // tm
    return pl.pallas_call(
        _ln_mod_kernel,
        grid=(t // tm,),
        in_specs=[pl.BlockSpec((tm, d), lambda i: (i, 0)),
                  _mod_spec(d, shift_comp, tps, row0),
                  _mod_spec(d, scale_comp, tps, row0)],
        out_specs=pl.BlockSpec((tm, d), lambda i: (i, 0)),
        out_shape=jax.ShapeDtypeStruct((t, d), BF16),
        compiler_params=_cparams(("parallel",)),
        name="ln_mod",
    )(x, mod3, mod3)


def _proj_kernel(*refs, rope, scale, n_out):
    x_ref, w_ref = refs[0], refs[1]
    outs = refs[len(refs) - n_out:]
    acc = _dot(x_ref[...], w_ref[...])
    if rope:
        c_ref, s1_ref, s2_ref = refs[2:5]
        cos, s1, s2 = c_ref[...], s1_ref[...], s2_ref[...]
        parts = []
        for g in range(acc.shape[1] // LANES):
            blk = acc[:, g * LANES:(g + 1) * LANES]
            half = ROPE_DIMS // 2
            parts.append(blk * cos + pltpu.roll(blk, LANES - half, 1) * s1 + pltpu.roll(blk, half, 1) * s2)
        acc = jnp.concatenate(parts, axis=1) if len(parts) > 1 else parts[0]
    if scale != 1.0:
        acc = acc * scale
    for o in outs:
        o[...] = acc.astype(o.dtype)


def _proj(x, w, out_dtypes, rope_tabs=None, scale=1.0, tm_pref=512, tn_pref=1024):
    m, k = x.shape
    n = w.shape[1]
    tm, tn = _tile(m, tm_pref), _tile(n, tn_pref)
    in_specs = [pl.BlockSpec((tm, k), lambda i, j: (i, 0)),
                pl.BlockSpec((k, tn), lambda i, j: (0, j))]
    args = [x, w]
    if rope_tabs is not None:
        in_specs += [pl.BlockSpec((tm, LANES), lambda i, j: (i, 0))] * 3
        args += list(rope_tabs)
    outs = pl.pallas_call(
        functools.partial(_proj_kernel, rope=rope_tabs is not None, scale=scale, n_out=len(out_dtypes)),
        grid=(m // tm, n // tn),
        in_specs=in_specs,
        out_specs=[pl.BlockSpec((tm, tn), lambda i, j: (i, j)) for _ in out_dtypes],
        out_shape=[jax.ShapeDtypeStruct((m, n), dt) for dt in out_dtypes],
        compiler_params=_cparams(("parallel", "arbitrary")),
        name="proj",
    )(*args)
    return outs


def _glu_kernel(x_ref, wg_ref, wu_ref, o_ref):
    x = x_ref[...]
    g = _dot(x, wg_ref[...])
    u = _dot(x, wu_ref[...])
    o_ref[...] = (g * jax.nn.sigmoid(g) * u).astype(o_ref.dtype)


def _glu(x, wg, wu):
    m, k = x.shape
    n = wg.shape[1]
    tm, tn = _tile(m, 512), _tile(n, 512)
    return pl.pallas_call(
        _glu_kernel,
        grid=(m // tm, n // tn),
        in_specs=[pl.BlockSpec((tm, k), lambda i, j: (i, 0)),
                  pl.BlockSpec((k, tn), lambda i, j: (0, j)),
                  pl.BlockSpec((k, tn), lambda i, j: (0, j))],
        out_specs=pl.BlockSpec((tm, tn), lambda i, j: (i, j)),
        out_shape=jax.ShapeDtypeStruct((m, n), BF16),
        compiler_params=_cparams(("parallel", "arbitrary")),
        name="shared_glu",
    )(x, wg, wu)


def _merge_kernel(od_ref, os_ref, wd_ref, ws_ref, gd_ref, gs_ref, o_ref):
    bd = _dot(od_ref[...], wd_ref[...])
    bs = _dot(os_ref[...], ws_ref[...])
    gd = jax.nn.sigmoid(gd_ref[...].astype(F32))
    gs = jax.nn.sigmoid(gs_ref[...].astype(F32))
    o_ref[...] = (gd * bd + gs * bs).astype(o_ref.dtype)


def _merge(od, osb, wbd, wbs, gd, gs):
    m, k = od.shape
    n = wbd.shape[1]
    tm, tn = _tile(m, 512), _tile(n, 1024)
    return pl.pallas_call(
        _merge_kernel,
        grid=(m // tm, n // tn),
        in_specs=[pl.BlockSpec((tm, k), lambda i, j: (i, 0)),
                  pl.BlockSpec((tm, k), lambda i, j: (i, 0)),
                  pl.BlockSpec((k, tn), lambda i, j: (0, j)),
                  pl.BlockSpec((k, tn), lambda i, j: (0, j)),
                  pl.BlockSpec((tm, tn), lambda i, j: (i, j)),
                  pl.BlockSpec((tm, tn), lambda i, j: (i, j))],
        out_specs=pl.BlockSpec((tm, tn), lambda i, j: (i, j)),
        out_shape=jax.ShapeDtypeStruct((m, n), BF16),
        compiler_params=_cparams(("parallel", "arbitrary")),
        name="merge",
    )(od, osb, wbd, wbs, gd, gs)


def _lambda(lam_ref, lam_init):
    a = jnp.sum(lam_ref[0:1, :] * lam_ref[1:2, :], axis=-1, keepdims=True)
    b = jnp.sum(lam_ref[2:3, :] * lam_ref[3:4, :], axis=-1, keepdims=True)
    return jnp.exp(a) - jnp.exp(b) + lam_init


def _split_components(q):
    lane = lax.broadcasted_iota(jnp.int32, q.shape, 1)
    zero = jnp.zeros_like(q)
    return jnp.where(lane < DIFF_DK, q, zero), jnp.where(lane >= DIFF_DK, q, zero)


def _diff_finish(o1, l1, o2, l2, lam, g, lam_init):
    o = o1 / l1 - lam * (o2 / l2)
    o = o * lax.rsqrt(jnp.mean(o * o, axis=-1, keepdims=True) + LN_EPS)
    return o * g * (1.0 - lam_init)


def _diff_prompt_kernel(lam_ref, g_ref, q_ref, k_ref, v_ref, o_ref, m_ref, l_ref, acc_ref, *, tq, tk, lam_init):
    i = pl.program_id(1)
    q1, q2 = _split_components(q_ref[...])
    m_ref[...] = jnp.full(m_ref.shape, -jnp.inf, F32)
    l_ref[...] = jnp.zeros(l_ref.shape, F32)
    acc_ref[...] = jnp.zeros(acc_ref.shape, F32)

    def block(start, keep):
        k = k_ref[pl.ds(start, tk), :]
        v = v_ref[pl.ds(start, tk), :]
        for comp, qc in enumerate((q1, q2)):
            s = _dot_t(qc, k)
            if keep is not None:
                s = jnp.where(keep[0], jnp.where(keep[1], s, -1e30), -1e30)
            m_old = m_ref[comp]
            m_new = jnp.maximum(m_old, jnp.max(s, axis=-1, keepdims=True))
            alpha = jnp.exp(m_old - m_new)
            p = jnp.exp(s - m_new)
            l_ref[comp] = alpha * l_ref[comp] + jnp.sum(p, axis=-1, keepdims=True)
            acc_ref[comp] = alpha * acc_ref[comp] + _dot(p.astype(BF16), v)
            m_ref[comp] = m_new

    n_full = (i * tq) // tk

    def body(j, carry):
        block(pl.multiple_of(j * tk, tk), None)
        return carry

    lax.fori_loop(0, n_full, body, 0)
    start = pl.multiple_of(jnp.maximum((i + 1) * tq - tk, 0), tq)
    lo = n_full * tk
    col = start + lax.broadcasted_iota(jnp.int32, (tq, tk), 1)
    row = i * tq + lax.broadcasted_iota(jnp.int32, (tq, 1), 0)
    hi = (row // CHUNK + 1) * CHUNK
    block(start, (col >= lo, col < hi))
    lam = _lambda(lam_ref, lam_init)
    o = _diff_finish(acc_ref[0], l_ref[0], acc_ref[1], l_ref[1], lam, g_ref[...], lam_init)
    o_ref[...] = o.astype(o_ref.dtype)


def _diff_prompt(q, k, v, lam4, g, lam_init):
    t, w = q.shape
    heads = w // HEAD_W
    tq = _tile(t, 256)
    tk = _tile(t, 1024)
    assert tq % CHUNK == 0 and tk % tq == 0
    return pl.pallas_call(
        functools.partial(_diff_prompt_kernel, tq=tq, tk=tk, lam_init=lam_init),
        grid=(heads, t // tq),
        in_specs=[pl.BlockSpec((4, DIFF_DK), lambda h, i: (0, 0)),
                  pl.BlockSpec((1, HEAD_W), lambda h, i: (0, 0)),
                  pl.BlockSpec((tq, HEAD_W), lambda h, i: (i, h)),
                  pl.BlockSpec((t, HEAD_W), lambda h, i: (0, h)),
                  pl.BlockSpec((t, HEAD_W), lambda h, i: (0, h))],
        out_specs=pl.BlockSpec((tq, HEAD_W), lambda h, i: (i, h)),
        out_shape=jax.ShapeDtypeStruct((t, w), BF16),
        scratch_shapes=[pltpu.VMEM((2, tq, 1), F32), pltpu.VMEM((2, tq, 1), F32),
                        pltpu.VMEM((2, tq, HEAD_W), F32)],
        compiler_params=_cparams(("parallel", "arbitrary")),
        name="diff_prompt",
    )(lam4, g, q, k, v)


def _diff_sample_kernel(lam_ref, g_ref, q_ref, kn_ref, vn_ref, kct_ref, vc_ref, o_ref, *, past_len, lam_init):
    q1, q2 = _split_components(q_ref[...])
    kpt = kct_ref[...].astype(BF16)
    vp = vc_ref[...].astype(BF16)
    kn = kn_ref[...].astype(BF16)
    vn = vn_ref[...].astype(BF16)
    tn = kn.shape[0]
    r = (past_len + lax.broadcasted_iota(jnp.int32, (tn, tn), 0)) // CHUNK
    c = (past_len + lax.broadcasted_iota(jnp.int32, (tn, tn), 1)) // CHUNK
    keep = c <= r
    res = []
    for qc in (q1, q2):
        sp = _dot(qc, kpt)
        sn = jnp.where(keep, _dot_t(qc, kn), -1e30)
        m = jnp.maximum(jnp.max(sp, axis=-1, keepdims=True), jnp.max(sn, axis=-1, keepdims=True))
        pp = jnp.exp(sp - m)
        pn = jnp.exp(sn - m)
        l = jnp.sum(pp, axis=-1, keepdims=True) + jnp.sum(pn, axis=-1, keepdims=True)
        res.append((_dot(pp.astype(BF16), vp) + _dot(pn.astype(BF16), vn), l))
    lam = _lambda(lam_ref, lam_init)
    o = _diff_finish(res[0][0], res[0][1], res[1][0], res[1][1], lam, g_ref[...], lam_init)
    o_ref[...] = o.astype(o_ref.dtype)


def _diff_sample(q, kn, vn, kct, vc, lam4, g, lam_init, seq_len):
    t, w = q.shape
    b, heads, _, past_len = kct.shape
    assert past_len % CHUNK == 0
    row = pl.BlockSpec((seq_len, HEAD_W), lambda bi, h: (bi, h))
    cache = pl.BlockSpec((None, past_len, HEAD_W), lambda bi, h: (bi, 0, h))
    cache_t = pl.BlockSpec((None, None, HEAD_W, past_len), lambda bi, h: (bi, h, 0, 0))
    return pl.pallas_call(
        functools.partial(_diff_sample_kernel, past_len=past_len, lam_init=lam_init),
        grid=(b, heads),
        in_specs=[pl.BlockSpec((4, DIFF_DK), lambda bi, h: (0, 0)),
                  pl.BlockSpec((1, HEAD_W), lambda bi, h: (0, 0)),
                  row, row, row, cache_t, cache],
        out_specs=row,
        out_shape=jax.ShapeDtypeStruct((t, w), BF16),
        compiler_params=_cparams(("parallel", "arbitrary")),
        name="diff_sample",
    )(lam4, g, q, kn, vn, kct, vc)


def _head_rows(cache_ref, head, start, size, heads):
    return cache_ref[pl.ds(start * heads + head, size, stride=heads), :]


def _head_major(x, b, s):
    return x.reshape(b, s, -1, HEAD_W).transpose(0, 2, 1, 3)


def _token_major(x):
    b, h, s, w = x.shape
    return x.transpose(0, 2, 1, 3).reshape(b * s, h * w)


def _sample_specs(seq_len, heads, pc, chunk_index):
    row = pl.BlockSpec((None, heads, seq_len, HEAD_W), lambda bi, c: (bi, 0, 0, 0))
    cache = pl.BlockSpec((None, pc * heads, HEAD_W), lambda bi, c: (bi, chunk_index(c), 0))
    return row, cache


def _sb_block(q, k, v, carry_ref, acc_ref, keep):
    tk = k.shape[0]
    z = _dot_t(q, k)
    soft = jnp.log(1.0 + jnp.exp(-jnp.abs(z)))
    log_sig = jnp.minimum(z, 0.0) - soft
    log_keep = jnp.minimum(-z, 0.0) - soft
    if keep is not None:
        log_keep = jnp.where(keep, log_keep, 0.0)
    upper = (lax.broadcasted_iota(jnp.int32, (tk, tk), 0) > lax.broadcasted_iota(jnp.int32, (tk, tk), 1))
    upper = jnp.where(upper, 1.0, 0.0).astype(BF16)
    hi = log_keep.astype(BF16)
    lo = (log_keep - hi.astype(F32)).astype(BF16)
    later = _dot(hi, upper) + _dot(lo, upper) + carry_ref[...]
    w = jnp.exp(log_sig + later)
    if keep is not None:
        w = jnp.where(keep, w, 0.0)
    acc_ref[...] += _dot(w.astype(BF16), v)
    carry_ref[...] += jnp.sum(log_keep, axis=-1, keepdims=True)


def _sb_prompt_kernel(q_ref, k_ref, v_ref, o_ref, carry_ref, acc_ref, *, tq):
    i = pl.program_id(1)
    q = q_ref[...]
    carry_ref[...] = jnp.zeros(carry_ref.shape, F32)
    acc_ref[...] = jnp.zeros(acc_ref.shape, F32)

    def load(j):
        start = pl.multiple_of(j * tq, tq)
        return k_ref[pl.ds(start, tq), :], v_ref[pl.ds(start, tq), :]

    strict = (lax.broadcasted_iota(jnp.int32, (tq, tq), 1) < lax.broadcasted_iota(jnp.int32, (tq, tq), 0))
    k, v = load(i)
    _sb_block(q, k, v, carry_ref, acc_ref, strict)

    def cond(state):
        j, alive = state
        return jnp.logical_and(j >= 0, alive > SB_DEAD_LOG)

    def body(state):
        j, _ = state
        kj, vj = load(j)
        _sb_block(q, kj, vj, carry_ref, acc_ref, None)
        return j - 1, jnp.max(carry_ref[...])

    lax.while_loop(cond, body, (i - 1, jnp.max(carry_ref[...])))
    o_ref[...] = acc_ref[...].astype(o_ref.dtype)


def _sb_prompt(q, k, v):
    t, w = q.shape
    heads = w // HEAD_W
    tq = _tile(t, 256)
    return pl.pallas_call(
        functools.partial(_sb_prompt_kernel, tq=tq),
        grid=(heads, t // tq),
        in_specs=[pl.BlockSpec((tq, HEAD_W), lambda h, i: (i, h)),
                  pl.BlockSpec((t, HEAD_W), lambda h, i: (0, h)),
                  pl.BlockSpec((t, HEAD_W), lambda h, i: (0, h))],
        out_specs=pl.BlockSpec((tq, HEAD_W), lambda h, i: (i, h)),
        out_shape=jax.ShapeDtypeStruct((t, w), BF16),
        scratch_shapes=[pltpu.VMEM((tq, 1), F32), pltpu.VMEM((tq, HEAD_W), F32)],
        compiler_params=_cparams(("parallel", "arbitrary")),
        name="sb_prompt",
    )(q, k, v)


def _sb_sample_kernel(q_ref, kn_ref, vn_ref, kc_ref, vc_ref, o_ref, carry_ref, acc_ref, *, heads, pc, tk):
    c = pl.program_id(1)
    tn = q_ref.shape[1]

    @pl.when(c == 0)
    def _():
        carry_ref[...] = jnp.zeros(carry_ref.shape, F32)
        acc_ref[...] = jnp.zeros(acc_ref.shape, F32)
        strict = (lax.broadcasted_iota(jnp.int32, (tn, tn), 1) < lax.broadcasted_iota(jnp.int32, (tn, tn), 0))

        def new_rows(h, carry):
            _sb_block(q_ref[h], kn_ref[h].astype(BF16), vn_ref[h].astype(BF16), carry_ref.at[h], acc_ref.at[h], strict)
            return carry

        lax.fori_loop(0, heads, new_rows, 0)

    for j in reversed(range(pc // tk)):
        @pl.when(jnp.max(carry_ref[...]) > SB_DEAD_LOG)
        def _():
            def past_rows(h, carry):
                kj = _head_rows(kc_ref, h, j * tk, tk, heads).astype(BF16)
                vj = _head_rows(vc_ref, h, j * tk, tk, heads).astype(BF16)
                _sb_block(q_ref[h], kj, vj, carry_ref.at[h], acc_ref.at[h], None)
                return carry

            lax.fori_loop(0, heads, past_rows, 0)

    @pl.when(c == pl.num_programs(1) - 1)
    def _():
        o_ref[...] = acc_ref[...].astype(o_ref.dtype)


def _sb_sample(q, kn, vn, kc, vc, seq_len):
    t, w = q.shape
    heads = w // HEAD_W
    b = kc.shape[0]
    past_len = kc.shape[1] // heads
    pc = _tile(past_len, SAMPLE_PAST_CHUNK)
    n_c = past_len // pc
    tk = _tile(pc, 256)
    row, cache = _sample_specs(seq_len, heads, pc, lambda c: n_c - 1 - c)
    out = pl.pallas_call(
        functools.partial(_sb_sample_kernel, heads=heads, pc=pc, tk=tk),
        grid=(b, n_c),
        in_specs=[row, row, row, cache, cache],
        out_specs=row,
        out_shape=jax.ShapeDtypeStruct((b, heads, seq_len, HEAD_W), BF16),
        scratch_shapes=[pltpu.VMEM((heads, seq_len, 1), F32), pltpu.VMEM((heads, seq_len, HEAD_W), F32)],
        compiler_params=_cparams(("parallel", "arbitrary")),
        name="sb_sample",
    )(_head_major(q, b, seq_len), _head_major(kn, b, seq_len), _head_major(vn, b, seq_len), kc, vc)
    return _token_major(out)


def _pack_halves(x):
    n = x.shape[1] // 2
    bits = pltpu.bitcast(x.astype(BF16).astype(F32), jnp.uint32)
    return bits[:, n:] | (bits[:, :n] >> 16)


def _unpack_halves(u):
    lo = pltpu.bitcast(u << 16, F32)
    hi = pltpu.bitcast(u & jnp.uint32(0xFFFF0000), F32)
    return lo, hi


def _resid_norm_kernel(x_ref, a_ref, gate_ref, g_ref, b_ref, shift_ref, scale_ref, x1_ref, h_ref, hf_ref, *, alpha):
    x1 = _ln(alpha * x_ref[...] + gate_ref[...] * a_ref[...]) * g_ref[...] + b_ref[...]
    x1_ref[...] = x1
    h = _ln(x1) * (1.0 + scale_ref[...]) + shift_ref[...]
    h_ref[...] = h.astype(h_ref.dtype)
    hf_ref[...] = h


def _resid_norm(x, a, mod3, row0, seq_len, g, b, alpha):
    t, d = x.shape
    tm = _tile(seq_len, 256)
    tps = seq_len // tm
    tok = pl.BlockSpec((tm, d), lambda i: (i, 0))
    vec = pl.BlockSpec((1, d), lambda i: (0, 0))
    return pl.pallas_call(
        functools.partial(_resid_norm_kernel, alpha=alpha),
        grid=(t // tm,),
        in_specs=[tok, tok, _mod_spec(d, 2, tps, row0), vec, vec,
                  _mod_spec(d, 3, tps, row0), _mod_spec(d, 4, tps, row0)],
        out_specs=[tok, tok, tok],
        out_shape=[jax.ShapeDtypeStruct((t, d), F32), jax.ShapeDtypeStruct((t, d), BF16),
                   jax.ShapeDtypeStruct((t, d), F32)],
        compiler_params=_cparams(("parallel",)),
        name="resid_norm",
    )(x, a, mod3, g, b, mod3, mod3)


def _router_kernel(h_ref, whi_ref, wlo_ref, bias_ref, gate_ref, idx_ref, rank_ref, cnt_ref, run_ref):
    @pl.when(pl.program_id(0) == 0)
    def _():
        run_ref[...] = jnp.zeros(run_ref.shape, F32)

    h = h_ref[...]
    logits = _dot(h, whi_ref[...]) + _dot(h, wlo_ref[...])
    scores = jax.nn.sigmoid(logits)
    tm, e = scores.shape
    lane = lax.broadcasted_iota(jnp.int32, (tm, e), 1)
    sel = scores + bias_ref[...]
    picked = jnp.zeros((tm, e), F32)
    idx = jnp.zeros((tm, e), jnp.int32)
    firsts = []
    for k in range(TOP_K):
        best = jnp.max(sel, axis=-1, keepdims=True)
        first = jnp.min(jnp.where(sel == best, lane, e), axis=-1, keepdims=True)
        hit = lane == first
        picked = jnp.where(hit, 1.0, picked)
        sel = jnp.where(hit, -jnp.inf, sel)
        idx = jnp.where(lane == k, first, idx)
        firsts.append(first)
    gate = scores * picked
    gate = gate / jnp.sum(gate, axis=-1, keepdims=True) * ROUTE_SCALE
    lower = (lax.broadcasted_iota(jnp.int32, (tm, tm), 1) < lax.broadcasted_iota(jnp.int32, (tm, tm), 0))
    within = _dot(jnp.where(lower, 1.0, 0.0).astype(BF16), picked.astype(BF16)) + run_ref[...]
    gate8 = jnp.zeros((tm, e), F32)
    rank8 = jnp.zeros((tm, e), F32)
    for k, first in enumerate(firsts):
        hit = lane == first
        gk = jnp.sum(jnp.where(hit, gate, 0.0), axis=-1, keepdims=True)
        rk = jnp.sum(jnp.where(hit, within, 0.0), axis=-1, keepdims=True)
        gate8 = jnp.where(lane == k, gk, gate8)
        rank8 = jnp.where(lane == k, rk, rank8)
    gate_ref[...] = gate8
    idx_ref[...] = idx
    rank_ref[...] = rank8.astype(jnp.int32)
    run_ref[...] += jnp.sum(picked, axis=0, keepdims=True)
    cnt_ref[...] = run_ref[...].astype(jnp.int32)


def _router(h, w_router, router_bias):
    t, d = h.shape
    e = w_router.shape[1]
    assert e == LANES
    tm = _tile(t, 512)
    whi = w_router.astype(BF16)
    wlo = (w_router - whi.astype(F32)).astype(BF16)
    tok = pl.BlockSpec((tm, e), lambda i: (i, 0))
    wspec = pl.BlockSpec((d, e), lambda i: (0, 0))
    one = pl.BlockSpec((1, e), lambda i: (0, 0))
    return pl.pallas_call(
        _router_kernel,
        grid=(t // tm,),
        in_specs=[pl.BlockSpec((tm, d), lambda i: (i, 0)), wspec, wspec, one],
        out_specs=[tok, tok, tok, one],
        out_shape=[jax.ShapeDtypeStruct((t, e), F32), jax.ShapeDtypeStruct((t, e), jnp.int32),
                   jax.ShapeDtypeStruct((t, e), jnp.int32), jax.ShapeDtypeStruct((1, e), jnp.int32)],
        scratch_shapes=[pltpu.VMEM((1, e), F32)],
        compiler_params=_cparams(("arbitrary",)),
        name="router",
    )(h, whi, wlo, router_bias.reshape(1, e))


def _row_copy(src_hbm, tok, buf, slot, r, sem):
    return pltpu.make_async_copy(src_hbm.at[pl.ds(tok, 1)], buf.at[slot, pl.ds(r, 1)], sem.at[slot])


def _expert_up_kernel(bexp_ref, nused_ref, tok_ref, tok_next_ref, h_hbm, wg_ref, wu_ref, o_ref, buf, sem, *,
                      blk, n_blocks):
    b = pl.program_id(0)
    n_used = nused_ref[0]
    slot = lax.rem(b, 2)

    def start_rows(toks, into):
        for r in range(blk):
            _row_copy(h_hbm, toks[0, r], buf, into, r, sem).start()

    def wait_rows(into):
        for r in range(blk):
            _row_copy(h_hbm, 0, buf, into, r, sem).wait()

    @pl.when(b == 0)
    def _():
        def body(r, carry):
            _row_copy(h_hbm, tok_ref[0, r], buf, slot, r, sem).start()
            return carry
        lax.fori_loop(0, blk, body, 0)

    @pl.when(b < n_used)
    def _():
        wait_rows(slot)
        start_rows(tok_next_ref, 1 - slot)
        x = buf[slot].astype(BF16)
        g = _dot(x, wg_ref[...].astype(BF16))
        u = _dot(x, wu_ref[...].astype(BF16))
        o_ref[...] = (g * jax.nn.sigmoid(g) * u).astype(o_ref.dtype)

        @pl.when(b == n_blocks - 1)
        def _():
            wait_rows(1 - slot)

    @pl.when(b == n_used)
    def _():
        wait_rows(slot)

    @pl.when(b >= n_used)
    def _():
        o_ref[...] = jnp.zeros(o_ref.shape, o_ref.dtype)


def _expert_up(h, wg, wu, slot_tok, block_expert, n_used, blk):
    e, d, hid = wg.shape
    n_blocks = block_expert.shape[0]
    wspec = pl.BlockSpec((None, d, hid), lambda b, be, nu: (be[b], 0, 0))
    tok3 = slot_tok.reshape(n_blocks, 1, blk)
    grid_spec = pltpu.PrefetchScalarGridSpec(
        num_scalar_prefetch=2,
        grid=(n_blocks,),
        in_specs=[pl.BlockSpec((None, 1, blk), lambda b, be, nu: (b, 0, 0), memory_space=pltpu.SMEM),
                  pl.BlockSpec((None, 1, blk), lambda b, be, nu: (jnp.minimum(b + 1, n_blocks - 1), 0, 0),
                               memory_space=pltpu.SMEM),
                  pl.BlockSpec(memory_space=pl.ANY), wspec, wspec],
        out_specs=pl.BlockSpec((blk, hid), lambda b, be, nu: (b, 0)),
        scratch_shapes=[pltpu.VMEM((2, blk, d), F32), pltpu.SemaphoreType.DMA((2,))],
    )
    return pl.pallas_call(
        functools.partial(_expert_up_kernel, blk=blk, n_blocks=n_blocks),
        grid_spec=grid_spec,
        out_shape=jax.ShapeDtypeStruct((n_blocks * blk, hid), BF16),
        compiler_params=_cparams(("arbitrary",)),
        name="expert_up",
    )(block_expert, n_used, tok3, tok3, h, wg, wu)


def _expert_down_kernel(bexp_ref, nused_ref, a_ref, wd_ref, o_ref):
    @pl.when(pl.program_id(0) < nused_ref[0])
    def _():
        o_ref[...] = _pack_halves(_dot(a_ref[...], wd_ref[...].astype(BF16)))

    @pl.when(pl.program_id(0) >= nused_ref[0])
    def _():
        o_ref[...] = jnp.zeros(o_ref.shape, o_ref.dtype)


def _expert_down(act, wd, block_expert, n_used, blk):
    e, hid, d = wd.shape
    n_blocks = block_expert.shape[0]
    grid_spec = pltpu.PrefetchScalarGridSpec(
        num_scalar_prefetch=2,
        grid=(n_blocks,),
        in_specs=[pl.BlockSpec((blk, hid), lambda b, be, nu: (b, 0)),
                  pl.BlockSpec((None, hid, d), lambda b, be, nu: (be[b], 0, 0))],
        out_specs=pl.BlockSpec((blk, d // 2), lambda b, be, nu: (b, 0)),
    )
    return pl.pallas_call(
        _expert_down_kernel,
        grid_spec=grid_spec,
        out_shape=jax.ShapeDtypeStruct((n_blocks * blk, d // 2), jnp.uint32),
        compiler_params=_cparams(("arbitrary",)),
        name="expert_down",
    )(block_expert, n_used, act, wd)


def _combine_kernel(dest_ref, dest_next_ref, y_hbm, gate_ref, sh_ref, x_ref, gf_ref, g_ref, b_ref, o_ref, buf, sem, *,
                    tt, n_steps, alpha):
    i = pl.program_id(0)
    slot = lax.rem(i, 2)

    def copy(src_row, into, r, k):
        return pltpu.make_async_copy(y_hbm.at[pl.ds(src_row, 1)], buf.at[into, k, pl.ds(r, 1)], sem.at[into])

    def start_rows(dref, into):
        for r in range(tt):
            for k in range(TOP_K):
                copy(dref[0, r * TOP_K + k], into, r, k).start()

    def wait_rows(into):
        for r in range(tt):
            for k in range(TOP_K):
                copy(0, into, r, k).wait()

    @pl.when(i == 0)
    def _():
        def body(r, carry):
            for k in range(TOP_K):
                copy(dest_ref[0, r * TOP_K + k], slot, r, k).start()
            return carry
        lax.fori_loop(0, tt, body, 0)

    wait_rows(slot)
    start_rows(dest_next_ref, 1 - slot)
    gate = gate_ref[...]
    n = buf.shape[-1]
    f_lo, f_hi = sh_ref[:, :n], sh_ref[:, n:]
    for k in range(TOP_K):
        y_lo, y_hi = _unpack_halves(buf[slot, k])
        f_lo = f_lo + y_lo * gate[:, k:k + 1]
        f_hi = f_hi + y_hi * gate[:, k:k + 1]
    f = jnp.concatenate([f_lo, f_hi], axis=1)
    o_ref[...] = _ln(alpha * x_ref[...] + gf_ref[...] * f) * g_ref[...] + b_ref[...]

    @pl.when(i == n_steps - 1)
    def _():
        wait_rows(1 - slot)


def _combine(y_slots, dest8, gate8, shared, x1, mod3, row0, seq_len, tok0, n_tok, g, b, alpha):
    d = x1.shape[1]
    tt = _tile(seq_len, 64)
    assert tok0 % tt == 0
    tile0 = tok0 // tt
    tps = seq_len // tt
    n_steps = n_tok // tt
    tok = pl.BlockSpec((tt, d), lambda i: (tile0 + i, 0))
    vec = pl.BlockSpec((1, d), lambda i: (0, 0))
    dest3 = dest8.reshape(-1, 1, tt * TOP_K)
    return pl.pallas_call(
        functools.partial(_combine_kernel, tt=tt, n_steps=n_steps, alpha=alpha),
        grid=(n_steps,),
        in_specs=[pl.BlockSpec((None, 1, tt * TOP_K), lambda i: (tile0 + i, 0, 0), memory_space=pltpu.SMEM),
                  pl.BlockSpec((None, 1, tt * TOP_K), lambda i: (tile0 + jnp.minimum(i + 1, n_steps - 1), 0, 0),
                               memory_space=pltpu.SMEM),
                  pl.BlockSpec(memory_space=pl.ANY),
                  pl.BlockSpec((tt, LANES), lambda i: (tile0 + i, 0)),
                  tok, tok,
                  pl.BlockSpec((None, 1, d), lambda i: ((row0 + i // tps) * 6 + 5, 0, 0)),
                  vec, vec],
        out_specs=pl.BlockSpec((tt, d), lambda i: (i, 0)),
        out_shape=jax.ShapeDtypeStruct((n_tok, d), F32),
        scratch_shapes=[pltpu.VMEM((2, TOP_K, tt, d // 2), jnp.uint32), pltpu.SemaphoreType.DMA((2,))],
        compiler_params=_cparams(("arbitrary",)),
        name="combine",
    )(dest3, dest3, y_slots, gate8, shared, x1, mod3, g, b)


def _rope_tables(pos):
    half = ROPE_DIMS // 2
    inv_freq = ROPE_THETA ** (-jnp.arange(half, dtype=F32) / half)
    ang = pos.astype(F32)[:, None] * inv_freq[None, :]
    cos, sin = jnp.cos(ang), jnp.sin(ang)
    n = pos.shape[0]
    ones = jnp.ones((n, DIFF_DK - ROPE_DIMS), F32)
    zeros = jnp.zeros((n, DIFF_DK - ROPE_DIMS), F32)
    zh = jnp.zeros((n, half), F32)
    c64 = jnp.concatenate([cos, cos, ones], axis=1)
    s1 = jnp.concatenate([-sin, zh, zeros], axis=1)
    s2 = jnp.concatenate([zh, sin, zeros], axis=1)
    rep = LANES // DIFF_DK
    return tuple(jnp.tile(a, (1, rep)) for a in (c64, s1, s2))


def _mixer(x, mod3, row0, seq_len, past, wts, lam4, g_norm, lam_init):
    (w_qd, w_kd, w_vd, w_qs, w_ks, w_vs, w_gd, w_gs, w_bd, w_bs, w_out) = wts
    t = x.shape[0]
    past_len = 0 if past is None else past[0].shape[-1]
    pos = past_len + jnp.arange(seq_len, dtype=jnp.int32)
    tabs = _rope_tables(jnp.tile(pos, t // seq_len))
    h = _ln_mod(x, mod3, row0, seq_len, 0, 1)
    (qd,) = _proj(h, w_qd, (BF16,), rope_tabs=tabs, scale=DIFF_DK ** -0.5)
    (qs,) = _proj(h, w_qs, (BF16,), scale=HEAD_W ** -0.5)
    (gd,) = _proj(h, w_gd, (BF16,))
    (gs,) = _proj(h, w_gs, (BF16,))
    if past is None:
        kd, kd_b = _proj(h, w_kd, (F32, BF16), rope_tabs=tabs)
        vd, vd_b = _proj(h, w_vd, (F32, BF16))
        ks, ks_b = _proj(h, w_ks, (F32, BF16))
        vs, vs_b = _proj(h, w_vs, (F32, BF16))
        od = _diff_prompt(qd, kd_b, vd_b, lam4, g_norm, lam_init)
        osb = _sb_prompt(qs, ks_b, vs_b)
    else:
        (kd,) = _proj(h, w_kd, (F32,), rope_tabs=tabs)
        (vd,) = _proj(h, w_vd, (F32,))
        (ks,) = _proj(h, w_ks, (F32,))
        (vs,) = _proj(h, w_vs, (F32,))
        od = _diff_sample(qd, kd, vd, past[0], past[1], lam4, g_norm, lam_init, seq_len)
        osb = _sb_sample(qs, ks, vs, past[2], past[3], seq_len)
    merged = _merge(od, osb, w_bd, w_bs, gd, gs)
    (a,) = _proj(merged, w_out, (F32,))
    return a, (kd, vd, ks, vs)


def _moe_plan(idx8, rank8, counts, n_tok, blk):
    e = counts.shape[0]
    padded = (counts + blk - 1) // blk * blk
    pad_ends = jnp.cumsum(padded)
    pad_starts = pad_ends - padded
    hit = idx8[..., None] == jnp.arange(e, dtype=jnp.int32)
    dest8 = (jnp.sum(jnp.where(hit, pad_starts, 0), axis=-1) + rank8).astype(jnp.int32)
    n_blocks = -(-n_tok * TOP_K // blk) + e
    tok_ids = jnp.repeat(jnp.arange(n_tok, dtype=jnp.int32), TOP_K)
    slot_tok = jnp.zeros((n_blocks * blk,), jnp.int32).at[dest8.reshape(-1)].set(tok_ids)
    block_pos = jnp.arange(n_blocks, dtype=jnp.int32) * blk
    block_expert = jnp.minimum(jnp.searchsorted(pad_ends, block_pos, side='right'), e - 1).astype(jnp.int32)
    n_used = (pad_ends[-1] // blk).astype(jnp.int32).reshape(1)
    return dest8, slot_tok, block_expert, n_used


def kernel(x_prompt, x_sample, c_prompt, c_sample, cache_diff_k, cache_diff_v, cache_sb_k, cache_sb_v, w_ada, b_ada, w_in, w_br_diff, w_br_sb, w_out, lam_q1, lam_k1, lam_q2, lam_k2, diff_norm_g, ln1_g, ln1_b, w_router, router_bias, w_e_gate, w_e_up, w_e_down, w_sh_gate, w_sh_up, w_sh_down, ln2_g, ln2_b):
    depth = w_ada.shape[0]
    assert depth == 1
    bp, sp, d = x_prompt.shape
    bs, ss, _ = x_sample.shape
    assert bp == 1
    diff_w = w_br_diff.shape[1]
    sb_w = w_br_sb.shape[1]
    heads_d, heads_s = diff_w // HEAD_W, sb_w // HEAD_W
    past_len = cache_diff_k.shape[2]
    alpha = (2.0 * depth) ** 0.25
    lam_init = 0.8 - 0.6 * float(np.exp(-0.3 * 0))
    blk = 256

    widths = [diff_w, diff_w, diff_w, sb_w, sb_w, sb_w, d, d]
    offs = np.concatenate([[0], np.cumsum(widths)])
    w_in0 = w_in[0]
    w_qd, w_kd, w_vd, w_qs, w_ks, w_vs, w_gd, w_gs = (
        w_in0[:, offs[i]:offs[i + 1]].astype(BF16) for i in range(8))
    wts = (w_qd, w_kd, w_vd, w_qs, w_ks, w_vs, w_gd, w_gs,
           w_br_diff[0].astype(BF16), w_br_sb[0].astype(BF16), w_out[0].astype(BF16))
    lam4 = jnp.stack([lam_q1[0], lam_k1[0], lam_q2[0], lam_k2[0]]).astype(F32)
    g_norm = diff_norm_g[0].reshape(1, HEAD_W).astype(F32)

    c_all = jnp.concatenate([c_prompt, c_sample], axis=0)
    n_seq = c_all.shape[0]
    c_pad = jnp.pad(c_all, ((0, -n_seq % 8), (0, 0)))
    mod = _ada(c_pad, w_ada[0], b_ada[0])
    mod3 = mod.reshape(c_pad.shape[0] * 6, 1, d)

    xp = x_prompt.reshape(bp * sp, d)
    xs = x_sample.reshape(bs * ss, d)
    past = (jnp.transpose(cache_diff_k[0], (0, 2, 3, 4, 1)).reshape(bs, heads_d, HEAD_W, past_len),
            cache_diff_v[0].reshape(bs, past_len, diff_w),
            cache_sb_k[0].reshape(bs, past_len * heads_s, HEAD_W), cache_sb_v[0].reshape(bs, past_len * heads_s, HEAD_W))

    a_p, rows_p = _mixer(xp, mod3, 0, sp, None, wts, lam4, g_norm, lam_init)
    a_s, rows_s = _mixer(xs, mod3, bp, ss, past, wts, lam4, g_norm, lam_init)
    g1, b1 = ln1_g[0].reshape(1, d), ln1_b[0].reshape(1, d)
    x1_p, h2_p, h2f_p = _resid_norm(xp, a_p, mod3, 0, sp, g1, b1, alpha)
    x1_s, h2_s, h2f_s = _resid_norm(xs, a_s, mod3, bp, ss, g1, b1, alpha)

    n_p, n_s = bp * sp, bs * ss
    n_tok = n_p + n_s
    h2 = jnp.concatenate([h2_p, h2_s], axis=0)
    h2f = jnp.concatenate([h2f_p, h2f_s], axis=0)
    x1 = jnp.concatenate([x1_p, x1_s], axis=0)
    gate8, idx8, rank8, counts = _router(h2, w_router[0], router_bias[0])
    dest8, slot_tok, block_expert, n_used = _moe_plan(idx8[:, :TOP_K], rank8[:, :TOP_K], counts[0], n_tok, blk)
    act = _expert_up(h2f, w_e_gate[0], w_e_up[0], slot_tok, block_expert, n_used, blk)
    y_slots = _expert_down(act, w_e_down[0], block_expert, n_used, blk)
    sh_act = _glu(h2, w_sh_gate[0].astype(BF16), w_sh_up[0].astype(BF16))
    (shared,) = _proj(sh_act, w_sh_down[0].astype(BF16), (F32,))
    g2, b2 = ln2_g[0].reshape(1, d), ln2_b[0].reshape(1, d)
    y_p = _combine(y_slots, dest8, gate8, shared, x1, mod3, 0, sp, 0, n_p, g2, b2, alpha)
    y_s = _combine(y_slots, dest8, gate8, shared, x1, mod3, bp, ss, n_p, n_s, g2, b2, alpha)

    def rows(r, b, s):
        kd, vd, ks, vs = r
        return (kd.reshape(1, b, s, heads_d, 2, DIFF_DK), vd.reshape(1, b, s, heads_d, HEAD_W),
                ks.reshape(1, b, s, heads_s, HEAD_W), vs.reshape(1, b, s, heads_s, HEAD_W))

    return (y_p.reshape(bp, sp, d), y_s.reshape(bs, ss, d)) + rows(rows_p, bp, sp) + rows(rows_s, bs, ss)
```

```python
import functools

import jax
import jax.numpy as jnp
import numpy as np
from jax import lax
from jax.experimental import pallas as pl
from jax.experimental.pallas import tpu as pltpu

F32 = jnp.float32
BF16 = jnp.bfloat16

CHUNK = 64
DIFF_DK = 64
HEAD_W = 128
ROPE_THETA = 500000.0
ROPE_DIMS = DIFF_DK // 4
TOP_K = 8
ROUTE_SCALE = 2.5
LN_EPS = 1e-5
LANES = 128
VMEM_LIMIT = 52 * 1024 * 1024
SB_DEAD_LOG = -104.0
SAMPLE_PAST_CHUNK = 1024


def _cparams(sem):
    return pltpu.CompilerParams(dimension_semantics=sem, vmem_limit_bytes=VMEM_LIMIT)


def _tile(n, pref):
    if n <= pref:
        return n
    t = pref - pref % 64
    while n % t:
        t -= 64
    assert t > 0, (n, pref)
    return t


def _ln(x):
    xc = x - jnp.mean(x, axis=-1, keepdims=True)
    return xc * lax.rsqrt(jnp.mean(xc * xc, axis=-1, keepdims=True) + LN_EPS)


def _dot(a, b):
    return jnp.dot(a, b, preferred_element_type=F32)


def _dot_t(a, b):
    return lax.dot_general(a, b, (((1,), (1,)), ((), ())), preferred_element_type=F32)


def _ada_kernel(c_ref, w_ref, b_ref, o_ref):
    c = c_ref[...]
    s = (c * jax.nn.sigmoid(c)).astype(BF16)
    o_ref[...] = _dot(s, w_ref[...].astype(BF16)) + b_ref[...]


def _ada(c, w_ada, b_ada):
    m, d = c.shape
    n = w_ada.shape[1]
    tn = _tile(n, 512)
    return pl.pallas_call(
        _ada_kernel,
        grid=(n // tn,),
        in_specs=[pl.BlockSpec((m, d), lambda j: (0, 0)),
                  pl.BlockSpec((d, tn), lambda j: (0, j)),
                  pl.BlockSpec((1, tn), lambda j: (0, j))],
        out_specs=pl.BlockSpec((m, tn), lambda j: (0, j)),
        out_shape=jax.ShapeDtypeStruct((m, n), F32),
        compiler_params=_cparams(("arbitrary",)),
        name="ada",
    )(c, w_ada, b_ada.reshape(1, n))


def _ln_mod_kernel(x_ref, shift_ref, scale_ref, h_ref):
    h_ref[...] = (_ln(x_ref[...]) * (1.0 + scale_ref[...]) + shift_ref[...]).astype(h_ref.dtype)


def _mod_spec(d, comp, tiles_per_seq, row0=0):
    return pl.BlockSpec((None, 1, d), lambda i: ((row0 + i // tiles_per_seq) * 6 + comp, 0, 0))


def _ln_mod(x, mod3, row0, seq_len, shift_comp, scale_comp):
    t, d = x.shape
    tm = _tile(seq_len, 256)
    tps = seq_len // tm
    return pl.pallas_call(
        _ln_mod_kernel,
        grid=(t // tm,),
        in_specs=[pl.BlockSpec((tm, d), lambda i: (i, 0)),
                  _mod_spec(d, shift_comp, tps, row0),
                  _mod_spec(d, scale_comp, tps, row0)],
        out_specs=pl.BlockSpec((tm, d), lambda i: (i, 0)),
        out_shape=jax.ShapeDtypeStruct((t, d), BF16),
        compiler_params=_cparams(("parallel",)),
        name="ln_mod",
    )(x, mod3, mod3)


def _proj_kernel(*refs, rope, scale, n_out):
    x_ref, w_ref = refs[0], refs[1]
    outs = refs[len(refs) - n_out:]
    acc = _dot(x_ref[...], w_ref[...])
    if rope:
        c_ref, s1_ref, s2_ref = refs[2:5]
        cos, s1, s2 = c_ref[...], s1_ref[...], s2_ref[...]
        parts = []
        for g in range(acc.shape[1] // LANES):
            blk = acc[:, g * LANES:(g + 1) * LANES]
            half = ROPE_DIMS // 2
            parts.append(blk * cos + pltpu.roll(blk, LANES - half, 1) * s1 + pltpu.roll(blk, half, 1) * s2)
        acc = jnp.concatenate(parts, axis=1) if len(parts) > 1 else parts[0]
    if scale != 1.0:
        acc = acc * scale
    for o in outs:
        o[...] = acc.astype(o.dtype)


def _proj(x, w, out_dtypes, rope_tabs=None, scale=1.0, tm_pref=512, tn_pref=1024):
    m, k = x.shape
    n = w.shape[1]
    tm, tn = _tile(m, tm_pref), _tile(n, tn_pref)
    in_specs = [pl.BlockSpec((tm, k), lambda i, j: (i, 0)),
                pl.BlockSpec((k, tn), lambda i, j: (0, j))]
    args = [x, w]
    if rope_tabs is not None:
        in_specs += [pl.BlockSpec((tm, LANES), lambda i, j: (i, 0))] * 3
        args += list(rope_tabs)
    outs = pl.pallas_call(
        functools.partial(_proj_kernel, rope=rope_tabs is not None, scale=scale, n_out=len(out_dtypes)),
        grid=(m // tm, n // tn),
        in_specs=in_specs,
        out_specs=[pl.BlockSpec((tm, tn), lambda i, j: (i, j)) for _ in out_dtypes],
        out_shape=[jax.ShapeDtypeStruct((m, n), dt) for dt in out_dtypes],
        compiler_params=_cparams(("parallel", "arbitrary")),
        name="proj",
    )(*args)
    return outs


def _glu_kernel(x_ref, wg_ref, wu_ref, o_ref):
    x = x_ref[...]
    g = _dot(x, wg_ref[...])
    u = _dot(x, wu_ref[...])
    o_ref[...] = (g * jax.nn.sigmoid(g) * u).astype(o_ref.dtype)


def _glu(x, wg, wu):
    m, k = x.shape
    n = wg.shape[1]
    tm, tn = _tile(m, 512), _tile(n, 512)
    return pl.pallas_call(
        _glu_kernel,
        grid=(m // tm, n // tn),
        in_specs=[pl.BlockSpec((tm, k), lambda i, j: (i, 0)),
                  pl.BlockSpec((k, tn), lambda i, j: (0, j)),
                  pl.BlockSpec((k, tn), lambda i, j: (0, j))],
        out_specs=pl.BlockSpec((tm, tn), lambda i, j: (i, j)),
        out_shape=jax.ShapeDtypeStruct((m, n), BF16),
        compiler_params=_cparams(("parallel", "arbitrary")),
        name="shared_glu",
    )(x, wg, wu)


def _merge_kernel(od_ref, os_ref, wd_ref, ws_ref, gd_ref, gs_ref, o_ref):
    bd = _dot(od_ref[...], wd_ref[...])
    bs = _dot(os_ref[...], ws_ref[...])
    gd = jax.nn.sigmoid(gd_ref[...].astype(F32))
    gs = jax.nn.sigmoid(gs_ref[...].astype(F32))
    o_ref[...] = (gd * bd + gs * bs).astype(o_ref.dtype)


def _merge(od, osb, wbd, wbs, gd, gs):
    m, k = od.shape
    n = wbd.shape[1]
    tm, tn = _tile(m, 512), _tile(n, 1024)
    return pl.pallas_call(
        _merge_kernel,
        grid=(m // tm, n // tn),
        in_specs=[pl.BlockSpec((tm, k), lambda i, j: (i, 0)),
                  pl.BlockSpec((tm, k), lambda i, j: (i, 0)),
                  pl.BlockSpec((k, tn), lambda i, j: (0, j)),
                  pl.BlockSpec((k, tn), lambda i, j: (0, j)),
                  pl.BlockSpec((tm, tn), lambda i, j: (i, j)),
                  pl.BlockSpec((tm, tn), lambda i, j: (i, j))],
        out_specs=pl.BlockSpec((tm, tn), lambda i, j: (i, j)),
        out_shape=jax.ShapeDtypeStruct((m, n), BF16),
        compiler_params=_cparams(("parallel", "arbitrary")),
        name="merge",
    )(od, osb, wbd, wbs, gd, gs)


def _lambda(lam_ref, lam_init):
    a = jnp.sum(lam_ref[0:1, :] * lam_ref[1:2, :], axis=-1, keepdims=True)
    b = jnp.sum(lam_ref[2:3, :] * lam_ref[3:4, :], axis=-1, keepdims=True)
    return jnp.exp(a) - jnp.exp(b) + lam_init


def _split_components(q):
    lane = lax.broadcasted_iota(jnp.int32, q.shape, 1)
    zero = jnp.zeros_like(q)
    return jnp.where(lane < DIFF_DK, q, zero), jnp.where(lane >= DIFF_DK, q, zero)


def _diff_finish(o1, l1, o2, l2, lam, g, lam_init):
    o = o1 / l1 - lam * (o2 / l2)
    o = o * lax.rsqrt(jnp.mean(o * o, axis=-1, keepdims=True) + LN_EPS)
    return o * g * (1.0 - lam_init)


def _diff_prompt_kernel(lam_ref, g_ref, q_ref, k_ref, v_ref, o_ref, m_ref, l_ref, acc_ref, *, tq, tk, lam_init):
    i = pl.program_id(1)
    q1, q2 = _split_components(q_ref[...])
    m_ref[...] = jnp.full(m_ref.shape, -jnp.inf, F32)
    l_ref[...] = jnp.zeros(l_ref.shape, F32)
    acc_ref[...] = jnp.zeros(acc_ref.shape, F32)

    def block(start, keep):
        k = k_ref[pl.ds(start, tk), :]
        v = v_ref[pl.ds(start, tk), :]
        for comp, qc in enumerate((q1, q2)):
            s = _dot_t(qc, k)
            if keep is not None:
                s = jnp.where(keep[0], jnp.where(keep[1], s, -1e30), -1e30)
            m_old = m_ref[comp]
            m_new = jnp.maximum(m_old, jnp.max(s, axis=-1, keepdims=True))
            alpha = jnp.exp(m_old - m_new)
            p = jnp.exp(s - m_new)
            l_ref[comp] = alpha * l_ref[comp] + jnp.sum(p, axis=-1, keepdims=True)
            acc_ref[comp] = alpha * acc_ref[comp] + _dot(p.astype(BF16), v)
            m_ref[comp] = m_new

    n_full = (i * tq) // tk

    def body(j, carry):
        block(pl.multiple_of(j * tk, tk), None)
        return carry

    lax.fori_loop(0, n_full, body, 0)
    start = pl.multiple_of(jnp.maximum((i + 1) * tq - tk, 0), tq)
    lo = n_full * tk
    col = start + lax.broadcasted_iota(jnp.int32, (tq, tk), 1)
    row = i * tq + lax.broadcasted_iota(jnp.int32, (tq, 1), 0)
    hi = (row // CHUNK + 1) * CHUNK
    block(start, (col >= lo, col < hi))
    lam = _lambda(lam_ref, lam_init)
    o = _diff_finish(acc_ref[0], l_ref[0], acc_ref[1], l_ref[1], lam, g_ref[...], lam_init)
    o_ref[...] = o.astype(o_ref.dtype)


def _diff_prompt(q, k, v, lam4, g, lam_init):
    t, w = q.shape
    heads = w // HEAD_W
    tq = _tile(t, 256)
    tk = _tile(t, 1024)
    assert tq % CHUNK == 0 and tk % tq == 0
    return pl.pallas_call(
        functools.partial(_diff_prompt_kernel, tq=tq, tk=tk, lam_init=lam_init),
        grid=(heads, t // tq),
        in_specs=[pl.BlockSpec((4, DIFF_DK), lambda h, i: (0, 0)),
                  pl.BlockSpec((1, HEAD_W), lambda h, i: (0, 0)),
                  pl.BlockSpec((tq, HEAD_W), lambda h, i: (i, h)),
                  pl.BlockSpec((t, HEAD_W), lambda h, i: (0, h)),
                  pl.BlockSpec((t, HEAD_W), lambda h, i: (0, h))],
        out_specs=pl.BlockSpec((tq, HEAD_W), lambda h, i: (i, h)),
        out_shape=jax.ShapeDtypeStruct((t, w), BF16),
        scratch_shapes=[pltpu.VMEM((2, tq, 1), F32), pltpu.VMEM((2, tq, 1), F32),
                        pltpu.VMEM((2, tq, HEAD_W), F32)],
        compiler_params=_cparams(("parallel", "arbitrary")),
        name="diff_prompt",
    )(lam4, g, q, k, v)


def _diff_sample_kernel(lam_ref, g_ref, q_ref, kn_ref, vn_ref, kct_ref, vc_ref, o_ref, *, past_len, lam_init):
    q1, q2 = _split_components(q_ref[...])
    kpt = kct_ref[...].astype(BF16)
    vp = vc_ref[...].astype(BF16)
    kn = kn_ref[...].astype(BF16)
    vn = vn_ref[...].astype(BF16)
    tn = kn.shape[0]
    r = (past_len + lax.broadcasted_iota(jnp.int32, (tn, tn), 0)) // CHUNK
    c = (past_len + lax.broadcasted_iota(jnp.int32, (tn, tn), 1)) // CHUNK
    keep = c <= r
    res = []
    for qc in (q1, q2):
        sp = _dot(qc, kpt)
        sn = jnp.where(keep, _dot_t(qc, kn), -1e30)
        m = jnp.maximum(jnp.max(sp, axis=-1, keepdims=True), jnp.max(sn, axis=-1, keepdims=True))
        pp = jnp.exp(sp - m)
        pn = jnp.exp(sn - m)
        l = jnp.sum(pp, axis=-1, keepdims=True) + jnp.sum(pn, axis=-1, keepdims=True)
        res.append((_dot(pp.astype(BF16), vp) + _dot(pn.astype(BF16), vn), l))
    lam = _lambda(lam_ref, lam_init)
    o = _diff_finish(res[0][0], res[0][1], res[1][0], res[1][1], lam, g_ref[...], lam_init)
    o_ref[...] = o.astype(o_ref.dtype)


def _diff_sample(q, kn, vn, kct, vc, lam4, g, lam_init, seq_len):
    t, w = q.shape
    b, heads, _, past_len = kct.shape
    assert past_len % CHUNK == 0
    row = pl.BlockSpec((seq_len, HEAD_W), lambda bi, h: (bi, h))
    cache = pl.BlockSpec((None, past_len, HEAD_W), lambda bi, h: (bi, 0, h))
    cache_t = pl.BlockSpec((None, None, HEAD_W, past_len), lambda bi, h: (bi, h, 0, 0))
    return pl.pallas_call(
        functools.partial(_diff_sample_kernel, past_len=past_len, lam_init=lam_init),
        grid=(b, heads),
        in_specs=[pl.BlockSpec((4, DIFF_DK), lambda bi, h: (0, 0)),
                  pl.BlockSpec((1, HEAD_W), lambda bi, h: (0, 0)),
                  row, row, row, cache_t, cache],
        out_specs=row,
        out_shape=jax.ShapeDtypeStruct((t, w), BF16),
        compiler_params=_cparams(("parallel", "arbitrary")),
        name="diff_sample",
    )(lam4, g, q, kn, vn, kct, vc)


def _head_rows(cache_ref, head, start, size, heads):
    return cache_ref[pl.ds(start * heads + head, size, stride=heads), :]


def _head_major(x, b, s):
    return x.reshape(b, s, -1, HEAD_W).transpose(0, 2, 1, 3)


def _token_major(x):
    b, h, s, w = x.shape
    return x.transpose(0, 2, 1, 3).reshape(b * s, h * w)


def _sample_specs(seq_len, heads, pc, chunk_index):
    row = pl.BlockSpec((None, heads, seq_len, HEAD_W), lambda bi, c: (bi, 0, 0, 0))
    cache = pl.BlockSpec((None, pc * heads, HEAD_W), lambda bi, c: (bi, chunk_index(c), 0))
    return row, cache


def _sb_block(q, k, v, carry_ref, acc_ref, keep):
    tk = k.shape[0]
    z = _dot_t(q, k)
    soft = jnp.log(1.0 + jnp.exp(-jnp.abs(z)))
    log_sig = jnp.minimum(z, 0.0) - soft
    log_keep = jnp.minimum(-z, 0.0) - soft
    if keep is not None:
        log_keep = jnp.where(keep, log_keep, 0.0)
    upper = (lax.broadcasted_iota(jnp.int32, (tk, tk), 0) > lax.broadcasted_iota(jnp.int32, (tk, tk), 1))
    upper = jnp.where(upper, 1.0, 0.0).astype(BF16)
    hi = log_keep.astype(BF16)
    lo = (log_keep - hi.astype(F32)).astype(BF16)
    later = _dot(hi, upper) + _dot(lo, upper) + carry_ref[...]
    w = jnp.exp(log_sig + later)
    if keep is not None:
        w = jnp.where(keep, w, 0.0)
    acc_ref[...] += _dot(w.astype(BF16), v)
    carry_ref[...] += jnp.sum(log_keep, axis=-1, keepdims=True)


def _sb_prompt_kernel(q_ref, k_ref, v_ref, o_ref, carry_ref, acc_ref, *, tq):
    i = pl.program_id(1)
    q = q_ref[...]
    carry_ref[...] = jnp.zeros(carry_ref.shape, F32)
    acc_ref[...] = jnp.zeros(acc_ref.shape, F32)

    def load(j):
        start = pl.multiple_of(j * tq, tq)
        return k_ref[pl.ds(start, tq), :], v_ref[pl.ds(start, tq), :]

    strict = (lax.broadcasted_iota(jnp.int32, (tq, tq), 1) < lax.broadcasted_iota(jnp.int32, (tq, tq), 0))
    k, v = load(i)
    _sb_block(q, k, v, carry_ref, acc_ref, strict)

    def cond(state):
        j, alive = state
        return jnp.logical_and(j >= 0, alive > SB_DEAD_LOG)

    def body(state):
        j, _ = state
        kj, vj = load(j)
        _sb_block(q, kj, vj, carry_ref, acc_ref, None)
        return j - 1, jnp.max(carry_ref[...])

    lax.while_loop(cond, body, (i - 1, jnp.max(carry_ref[...])))
    o_ref[...] = acc_ref[...].astype(o_ref.dtype)


def _sb_prompt(q, k, v):
    t, w = q.shape
    heads = w // HEAD_W
    tq = _tile(t, 256)
    return pl.pallas_call(
        functools.partial(_sb_prompt_kernel, tq=tq),
        grid=(heads, t // tq),
        in_specs=[pl.BlockSpec((tq, HEAD_W), lambda h, i: (i, h)),
                  pl.BlockSpec((t, HEAD_W), lambda h, i: (0, h)),
                  pl.BlockSpec((t, HEAD_W), lambda h, i: (0, h))],
        out_specs=pl.BlockSpec((tq, HEAD_W), lambda h, i: (i, h)),
        out_shape=jax.ShapeDtypeStruct((t, w), BF16),
        scratch_shapes=[pltpu.VMEM((tq, 1), F32), pltpu.VMEM((tq, HEAD_W), F32)],
        compiler_params=_cparams(("parallel", "arbitrary")),
        name="sb_prompt",
    )(q, k, v)


def _sb_sample_kernel(q_ref, kn_ref, vn_ref, kc_ref, vc_ref, o_ref, carry_ref, acc_ref, *, heads, pc, tk):
    c = pl.program_id(1)
    tn = q_ref.shape[1]

    @pl.when(c == 0)
    def _():
        carry_ref[...] = jnp.zeros(carry_ref.shape, F32)
        acc_ref[...] = jnp.zeros(acc_ref.shape, F32)
        strict = (lax.broadcasted_iota(jnp.int32, (tn, tn), 1) < lax.broadcasted_iota(jnp.int32, (tn, tn), 0))

        def new_rows(h, carry):
            _sb_block(q_ref[h], kn_ref[h].astype(BF16), vn_ref[h].astype(BF16), carry_ref.at[h], acc_ref.at[h], strict)
            return carry

        lax.fori_loop(0, heads, new_rows, 0)

    for j in reversed(range(pc // tk)):
        @pl.when(jnp.max(carry_ref[...]) > SB_DEAD_LOG)
        def _():
            def past_rows(h, carry):
                kj = _head_rows(kc_ref, h, j * tk, tk, heads).astype(BF16)
                vj = _head_rows(vc_ref, h, j * tk, tk, heads).astype(BF16)
                _sb_block(q_ref[h], kj, vj, carry_ref.at[h], acc_ref.at[h], None)
                return carry

            lax.fori_loop(0, heads, past_rows, 0)

    @pl.when(c == pl.num_programs(1) - 1)
    def _():
        o_ref[...] = acc_ref[...].astype(o_ref.dtype)


def _sb_sample(q, kn, vn, kc, vc, seq_len):
    t, w = q.shape
    heads = w // HEAD_W
    b = kc.shape[0]
    past_len = kc.shape[1] // heads
    pc = _tile(past_len, SAMPLE_PAST_CHUNK)
    n_c = past_len // pc
    tk = _tile(pc, 256)
    row, cache = _sample_specs(seq_len, heads, pc, lambda c: n_c - 1 - c)
    out = pl.pallas_call(
        functools.partial(_sb_sample_kernel, heads=heads, pc=pc, tk=tk),
        grid=(b, n_c),
        in_specs=[row, row, row, cache, cache],
        out_specs=row,
        out_shape=jax.ShapeDtypeStruct((b, heads, seq_len, HEAD_W), BF16),
        scratch_shapes=[pltpu.VMEM((heads, seq_len, 1), F32), pltpu.VMEM((heads, seq_len, HEAD_W), F32)],
        compiler_params=_cparams(("parallel", "arbitrary")),
        name="sb_sample",
    )(_head_major(q, b, seq_len), _head_major(kn, b, seq_len), _head_major(vn, b, seq_len), kc, vc)
    return _token_major(out)


def _pack_halves(x):
    n = x.shape[1] // 2
    bits = pltpu.bitcast(x.astype(BF16).astype(F32), jnp.uint32)
    return bits[:, n:] | (bits[:, :n] >> 16)


def _unpack_halves(u):
    lo = pltpu.bitcast(u << 16, F32)
    hi = pltpu.bitcast(u & jnp.uint32(0xFFFF0000), F32)
    return lo, hi


def _resid_norm_kernel(x_ref, a_ref, gate_ref, g_ref, b_ref, shift_ref, scale_ref, x1_ref, h_ref, hf_ref, *, alpha):
    x1 = _ln(alpha * x_ref[...] + gate_ref[...] * a_ref[...]) * g_ref[...] + b_ref[...]
    x1_ref[...] = x1
    h = _ln(x1) * (1.0 + scale_ref[...]) + shift_ref[...]
    h_ref[...] = h.astype(h_ref.dtype)
    hf_ref[...] = h


def _resid_norm(x, a, mod3, row0, seq_len, g, b, alpha):
    t, d = x.shape
    tm = _tile(seq_len, 256)
    tps = seq_len // tm
    tok = pl.BlockSpec((tm, d), lambda i: (i, 0))
    vec = pl.BlockSpec((1, d), lambda i: (0, 0))
    return pl.pallas_call(
        functools.partial(_resid_norm_kernel, alpha=alpha),
        grid=(t // tm,),
        in_specs=[tok, tok, _mod_spec(d, 2, tps, row0), vec, vec,
                  _mod_spec(d, 3, tps, row0), _mod_spec(d, 4, tps, row0)],
        out_specs=[tok, tok, tok],
        out_shape=[jax.ShapeDtypeStruct((t, d), F32), jax.ShapeDtypeStruct((t, d), BF16),
                   jax.ShapeDtypeStruct((t, d), F32)],
        compiler_params=_cparams(("parallel",)),
        name="resid_norm",
    )(x, a, mod3, g, b, mod3, mod3)


def _router_kernel(h_ref, whi_ref, wlo_ref, bias_ref, gate_ref, idx_ref, rank_ref, cnt_ref, run_ref):
    @pl.when(pl.program_id(0) == 0)
    def _():
        run_ref[...] = jnp.zeros(run_ref.shape, F32)

    h = h_ref[...]
    logits = _dot(h, whi_ref[...]) + _dot(h, wlo_ref[...])
    scores = jax.nn.sigmoid(logits)
    tm, e = scores.shape
    lane = lax.broadcasted_iota(jnp.int32, (tm, e), 1)
    sel = scores + bias_ref[...]
    picked = jnp.zeros((tm, e), F32)
    idx = jnp.zeros((tm, e), jnp.int32)
    firsts = []
    for k in range(TOP_K):
        best = jnp.max(sel, axis=-1, keepdims=True)
        first = jnp.min(jnp.where(sel == best, lane, e), axis=-1, keepdims=True)
        hit = lane == first
        picked = jnp.where(hit, 1.0, picked)
        sel = jnp.where(hit, -jnp.inf, sel)
        idx = jnp.where(lane == k, first, idx)
        firsts.append(first)
    gate = scores * picked
    gate = gate / jnp.sum(gate, axis=-1, keepdims=True) * ROUTE_SCALE
    lower = (lax.broadcasted_iota(jnp.int32, (tm, tm), 1) < lax.broadcasted_iota(jnp.int32, (tm, tm), 0))
    within = _dot(jnp.where(lower, 1.0, 0.0).astype(BF16), picked.astype(BF16)) + run_ref[...]
    gate8 = jnp.zeros((tm, e), F32)
    rank8 = jnp.zeros((tm, e), F32)
    for k, first in enumerate(firsts):
        hit = lane == first
        gk = jnp.sum(jnp.where(hit, gate, 0.0), axis=-1, keepdims=True)
        rk = jnp.sum(jnp.where(hit, within, 0.0), axis=-1, keepdims=True)
        gate8 = jnp.where(lane == k, gk, gate8)
        rank8 = jnp.where(lane == k, rk, rank8)
    gate_ref[...] = gate8
    idx_ref[...] = idx
    rank_ref[...] = rank8.astype(jnp.int32)
    run_ref[...] += jnp.sum(picked, axis=0, keepdims=True)
    cnt_ref[...] = run_ref[...].astype(jnp.int32)


def _router(h, w_router, router_bias):
    t, d = h.shape
    e = w_router.shape[1]
    assert e == LANES
    tm = _tile(t, 512)
    whi = w_router.astype(BF16)
    wlo = (w_router - whi.astype(F32)).astype(BF16)
    tok = pl.BlockSpec((tm, e), lambda i: (i, 0))
    wspec = pl.BlockSpec((d, e), lambda i: (0, 0))
    one = pl.BlockSpec((1, e), lambda i: (0, 0))
    return pl.pallas_call(
        _router_kernel,
        grid=(t // tm,),
        in_specs=[pl.BlockSpec((tm, d), lambda i: (i, 0)), wspec, wspec, one],
        out_specs=[tok, tok, tok, one],
        out_shape=[jax.ShapeDtypeStruct((t, e), F32), jax.ShapeDtypeStruct((t, e), jnp.int32),
                   jax.ShapeDtypeStruct((t, e), jnp.int32), jax.ShapeDtypeStruct((1, e), jnp.int32)],
        scratch_shapes=[pltpu.VMEM((1, e), F32)],
        compiler_params=_cparams(("arbitrary",)),
        name="router",
    )(h, whi, wlo, router_bias.reshape(1, e))


def _row_copy(src_hbm, tok, buf, slot, r, sem):
    return pltpu.make_async_copy(src_hbm.at[pl.ds(tok, 1)], buf.at[slot, pl.ds(r, 1)], sem.at[slot])


def _expert_weights(plan_ref, b, w_hbms, w_bufs, w_sem):
    expert, first, slot = plan_ref[0, b], plan_ref[1, b], plan_ref[2, b]
    has_next, nxt = plan_ref[3, b], plan_ref[4, b]

    def copies(e, into):
        return [pltpu.make_async_copy(w.at[e], buf.at[into], w_sem.at[into, n])
                for n, (w, buf) in enumerate(zip(w_hbms, w_bufs))]

    @pl.when(b == 0)
    def _():
        for cp in copies(expert, slot):
            cp.start()

    @pl.when(first == 1)
    def _():
        for cp in copies(expert, slot):
            cp.wait()

    @pl.when(jnp.logical_and(first == 1, has_next == 1))
    def _():
        for cp in copies(nxt, 1 - slot):
            cp.start()

    return slot


def _expert_up_kernel(plan_ref, nused_ref, tok_ref, tok_next_ref, h_hbm, wg_hbm, wu_hbm, o_ref, buf, sem,
                      wg_buf, wu_buf, w_sem, *, blk, n_blocks):
    b = pl.program_id(0)
    n_used = nused_ref[0]
    slot = lax.rem(b, 2)

    def start_rows(toks, into):
        for r in range(blk):
            _row_copy(h_hbm, toks[0, r], buf, into, r, sem).start()

    def wait_rows(into):
        for r in range(blk):
            _row_copy(h_hbm, 0, buf, into, r, sem).wait()

    @pl.when(b == 0)
    def _():
        def body(r, carry):
            _row_copy(h_hbm, tok_ref[0, r], buf, slot, r, sem).start()
            return carry
        lax.fori_loop(0, blk, body, 0)

    @pl.when(b < n_used)
    def _():
        ws = _expert_weights(plan_ref, b, (wg_hbm, wu_hbm), (wg_buf, wu_buf), w_sem)
        wait_rows(slot)
        start_rows(tok_next_ref, 1 - slot)
        x = buf[slot].astype(BF16)
        g = _dot(x, wg_buf[ws].astype(BF16))
        u = _dot(x, wu_buf[ws].astype(BF16))
        o_ref[...] = (g * jax.nn.sigmoid(g) * u).astype(o_ref.dtype)

        @pl.when(b == n_blocks - 1)
        def _():
            wait_rows(1 - slot)

    @pl.when(b == n_used)
    def _():
        wait_rows(slot)

    @pl.when(b >= n_used)
    def _():
        o_ref[...] = jnp.zeros(o_ref.shape, o_ref.dtype)


def _expert_up(h, wg, wu, slot_tok, plan, n_used, blk):
    e, d, hid = wg.shape
    n_blocks = plan.shape[1]
    tok3 = slot_tok.reshape(n_blocks, 1, blk)
    grid_spec = pltpu.PrefetchScalarGridSpec(
        num_scalar_prefetch=2,
        grid=(n_blocks,),
        in_specs=[pl.BlockSpec((None, 1, blk), lambda b, pr, nu: (b, 0, 0), memory_space=pltpu.SMEM),
                  pl.BlockSpec((None, 1, blk), lambda b, pr, nu: (jnp.minimum(b + 1, n_blocks - 1), 0, 0),
                               memory_space=pltpu.SMEM),
                  pl.BlockSpec(memory_space=pl.ANY), pl.BlockSpec(memory_space=pl.ANY),
                  pl.BlockSpec(memory_space=pl.ANY)],
        out_specs=pl.BlockSpec((blk, hid), lambda b, pr, nu: (b, 0)),
        scratch_shapes=[pltpu.VMEM((2, blk, d), F32), pltpu.SemaphoreType.DMA((2,)),
                        pltpu.VMEM((2, d, hid), F32), pltpu.VMEM((2, d, hid), F32),
                        pltpu.SemaphoreType.DMA((2, 2))],
    )
    return pl.pallas_call(
        functools.partial(_expert_up_kernel, blk=blk, n_blocks=n_blocks),
        grid_spec=grid_spec,
        out_shape=jax.ShapeDtypeStruct((n_blocks * blk, hid), BF16),
        compiler_params=_cparams(("arbitrary",)),
        name="expert_up",
    )(plan, n_used, tok3, tok3, h, wg, wu)


def _expert_down_kernel(plan_ref, nused_ref, a_ref, wd_hbm, o_ref, wd_buf, w_sem):
    b = pl.program_id(0)

    @pl.when(b < nused_ref[0])
    def _():
        ws = _expert_weights(plan_ref, b, (wd_hbm,), (wd_buf,), w_sem)
        o_ref[...] = _pack_halves(_dot(a_ref[...], wd_buf[ws].astype(BF16)))

    @pl.when(b >= nused_ref[0])
    def _():
        o_ref[...] = jnp.zeros(o_ref.shape, o_ref.dtype)


def _expert_down(act, wd, plan, n_used, blk):
    e, hid, d = wd.shape
    n_blocks = plan.shape[1]
    grid_spec = pltpu.PrefetchScalarGridSpec(
        num_scalar_prefetch=2,
        grid=(n_blocks,),
        in_specs=[pl.BlockSpec((blk, hid), lambda b, pr, nu: (b, 0)),
                  pl.BlockSpec(memory_space=pl.ANY)],
        out_specs=pl.BlockSpec((blk, d // 2), lambda b, pr, nu: (b, 0)),
        scratch_shapes=[pltpu.VMEM((2, hid, d), F32), pltpu.SemaphoreType.DMA((2, 1))],
    )
    return pl.pallas_call(
        _expert_down_kernel,
        grid_spec=grid_spec,
        out_shape=jax.ShapeDtypeStruct((n_blocks * blk, d // 2), jnp.uint32),
        compiler_params=_cparams(("arbitrary",)),
        name="expert_down",
    )(plan, n_used, act, wd)


def _combine_kernel(dest_ref, dest_next_ref, y_hbm, gate_ref, sh_ref, x_ref, gf_ref, g_ref, b_ref, o_ref, buf, sem, *,
                    tt, n_steps, alpha):
    i = pl.program_id(0)
    slot = lax.rem(i, 2)

    def copy(src_row, into, r, k):
        return pltpu.make_async_copy(y_hbm.at[pl.ds(src_row, 1)], buf.at[into, k, pl.ds(r, 1)], sem.at[into])

    def start_rows(dref, into):
        for r in range(tt):
            for k in range(TOP_K):
                copy(dref[0, r * TOP_K + k], into, r, k).start()

    def wait_rows(into):
        for r in range(tt):
            for k in range(TOP_K):
                copy(0, into, r, k).wait()

    @pl.when(i == 0)
    def _():
        def body(r, carry):
            for k in range(TOP_K):
                copy(dest_ref[0, r * TOP_K + k], slot, r, k).start()
            return carry
        lax.fori_loop(0, tt, body, 0)

    wait_rows(slot)
    start_rows(dest_next_ref, 1 - slot)
    gate = gate_ref[...]
    n = buf.shape[-1]
    f_lo, f_hi = sh_ref[:, :n], sh_ref[:, n:]
    for k in range(TOP_K):
        y_lo, y_hi = _unpack_halves(buf[slot, k])
        f_lo = f_lo + y_lo * gate[:, k:k + 1]
        f_hi = f_hi + y_hi * gate[:, k:k + 1]
    f = jnp.concatenate([f_lo, f_hi], axis=1)
    o_ref[...] = _ln(alpha * x_ref[...] + gf_ref[...] * f) * g_ref[...] + b_ref[...]

    @pl.when(i == n_steps - 1)
    def _():
        wait_rows(1 - slot)


def _combine(y_slots, dest8, gate8, shared, x1, mod3, row0, seq_len, tok0, n_tok, g, b, alpha):
    d = x1.shape[1]
    tt = _tile(seq_len, 64)
    assert tok0 % tt == 0
    tile0 = tok0 // tt
    tps = seq_len // tt
    n_steps = n_tok // tt
    tok = pl.BlockSpec((tt, d), lambda i: (tile0 + i, 0))
    vec = pl.BlockSpec((1, d), lambda i: (0, 0))
    dest3 = dest8.reshape(-1, 1, tt * TOP_K)
    return pl.pallas_call(
        functools.partial(_combine_kernel, tt=tt, n_steps=n_steps, alpha=alpha),
        grid=(n_steps,),
        in_specs=[pl.BlockSpec((None, 1, tt * TOP_K), lambda i: (tile0 + i, 0, 0), memory_space=pltpu.SMEM),
                  pl.BlockSpec((None, 1, tt * TOP_K), lambda i: (tile0 + jnp.minimum(i + 1, n_steps - 1), 0, 0),
                               memory_space=pltpu.SMEM),
                  pl.BlockSpec(memory_space=pl.ANY),
                  pl.BlockSpec((tt, LANES), lambda i: (tile0 + i, 0)),
                  tok, tok,
                  pl.BlockSpec((None, 1, d), lambda i: ((row0 + i // tps) * 6 + 5, 0, 0)),
                  vec, vec],
        out_specs=pl.BlockSpec((tt, d), lambda i: (i, 0)),
        out_shape=jax.ShapeDtypeStruct((n_tok, d), F32),
        scratch_shapes=[pltpu.VMEM((2, TOP_K, tt, d // 2), jnp.uint32), pltpu.SemaphoreType.DMA((2,))],
        compiler_params=_cparams(("arbitrary",)),
        name="combine",
    )(dest3, dest3, y_slots, gate8, shared, x1, mod3, g, b)


def _rope_tables(pos):
    half = ROPE_DIMS // 2
    inv_freq = ROPE_THETA ** (-jnp.arange(half, dtype=F32) / half)
    ang = pos.astype(F32)[:, None] * inv_freq[None, :]
    cos, sin = jnp.cos(ang), jnp.sin(ang)
    n = pos.shape[0]
    ones = jnp.ones((n, DIFF_DK - ROPE_DIMS), F32)
    zeros = jnp.zeros((n, DIFF_DK - ROPE_DIMS), F32)
    zh = jnp.zeros((n, half), F32)
    c64 = jnp.concatenate([cos, cos, ones], axis=1)
    s1 = jnp.concatenate([-sin, zh, zeros], axis=1)
    s2 = jnp.concatenate([zh, sin, zeros], axis=1)
    rep = LANES // DIFF_DK
    return tuple(jnp.tile(a, (1, rep)) for a in (c64, s1, s2))


def _mixer(x, mod3, row0, seq_len, past, wts, lam4, g_norm, lam_init):
    (w_qd, w_kd, w_vd, w_qs, w_ks, w_vs, w_gd, w_gs, w_bd, w_bs, w_out) = wts
    t = x.shape[0]
    past_len = 0 if past is None else past[0].shape[-1]
    pos = past_len + jnp.arange(seq_len, dtype=jnp.int32)
    tabs = _rope_tables(jnp.tile(pos, t // seq_len))
    h = _ln_mod(x, mod3, row0, seq_len, 0, 1)
    (qd,) = _proj(h, w_qd, (BF16,), rope_tabs=tabs, scale=DIFF_DK ** -0.5)
    (qs,) = _proj(h, w_qs, (BF16,), scale=HEAD_W ** -0.5)
    (gd,) = _proj(h, w_gd, (BF16,))
    (gs,) = _proj(h, w_gs, (BF16,))
    if past is None:
        kd, kd_b = _proj(h, w_kd, (F32, BF16), rope_tabs=tabs)
        vd, vd_b = _proj(h, w_vd, (F32, BF16))
        ks, ks_b = _proj(h, w_ks, (F32, BF16))
        vs, vs_b = _proj(h, w_vs, (F32, BF16))
        od = _diff_prompt(qd, kd_b, vd_b, lam4, g_norm, lam_init)
        osb = _sb_prompt(qs, ks_b, vs_b)
    else:
        (kd,) = _proj(h, w_kd, (F32,), rope_tabs=tabs)
        (vd,) = _proj(h, w_vd, (F32,))
        (ks,) = _proj(h, w_ks, (F32,))
        (vs,) = _proj(h, w_vs, (F32,))
        od = _diff_sample(qd, kd, vd, past[0], past[1], lam4, g_norm, lam_init, seq_len)
        osb = _sb_sample(qs, ks, vs, past[2], past[3], seq_len)
    merged = _merge(od, osb, w_bd, w_bs, gd, gs)
    (a,) = _proj(merged, w_out, (F32,))
    return a, (kd, vd, ks, vs)


def _moe_plan(idx8, rank8, counts, n_tok, blk):
    e = counts.shape[0]
    padded = (counts + blk - 1) // blk * blk
    pad_ends = jnp.cumsum(padded)
    pad_starts = pad_ends - padded
    hit = idx8[..., None] == jnp.arange(e, dtype=jnp.int32)
    dest8 = (jnp.sum(jnp.where(hit, pad_starts, 0), axis=-1) + rank8).astype(jnp.int32)
    n_blocks = -(-n_tok * TOP_K // blk) + e
    tok_ids = jnp.repeat(jnp.arange(n_tok, dtype=jnp.int32), TOP_K)
    slot_tok = jnp.zeros((n_blocks * blk,), jnp.int32).at[dest8.reshape(-1)].set(tok_ids)
    block_pos = jnp.arange(n_blocks, dtype=jnp.int32) * blk
    block_expert = jnp.minimum(jnp.searchsorted(pad_ends, block_pos, side='right'), e - 1).astype(jnp.int32)
    n_used = (pad_ends[-1] // blk).astype(jnp.int32).reshape(1)
    bidx = jnp.arange(n_blocks, dtype=jnp.int32)
    first = jnp.concatenate([jnp.ones((1,), bool), block_expert[1:] != block_expert[:-1]]) & (bidx < n_used[0])
    wslot = (jnp.cumsum(first.astype(jnp.int32)) - 1) % 2
    first_pos = jnp.where(first, bidx, n_blocks)
    later_first = jnp.concatenate([lax.cummin(first_pos, reverse=True)[1:], jnp.full((1,), n_blocks, jnp.int32)])
    has_next = later_first < n_blocks
    next_expert = block_expert[jnp.minimum(later_first, n_blocks - 1)]
    plan = jnp.stack([block_expert, first.astype(jnp.int32), wslot.astype(jnp.int32), has_next.astype(jnp.int32),
                      next_expert]).astype(jnp.int32)
    return dest8, slot_tok, plan, n_used


def kernel(x_prompt, x_sample, c_prompt, c_sample, cache_diff_k, cache_diff_v, cache_sb_k, cache_sb_v, w_ada, b_ada, w_in, w_br_diff, w_br_sb, w_out, lam_q1, lam_k1, lam_q2, lam_k2, diff_norm_g, ln1_g, ln1_b, w_router, router_bias, w_e_gate, w_e_up, w_e_down, w_sh_gate, w_sh_up, w_sh_down, ln2_g, ln2_b):
    depth = w_ada.shape[0]
    assert depth == 1
    bp, sp, d = x_prompt.shape
    bs, ss, _ = x_sample.shape
    assert bp == 1
    diff_w = w_br_diff.shape[1]
    sb_w = w_br_sb.shape[1]
    heads_d, heads_s = diff_w // HEAD_W, sb_w // HEAD_W
    past_len = cache_diff_k.shape[2]
    alpha = (2.0 * depth) ** 0.25
    lam_init = 0.8 - 0.6 * float(np.exp(-0.3 * 0))
    blk = 256

    widths = [diff_w, diff_w, diff_w, sb_w, sb_w, sb_w, d, d]
    offs = np.concatenate([[0], np.cumsum(widths)])
    w_in0 = w_in[0]
    w_qd, w_kd, w_vd, w_qs, w_ks, w_vs, w_gd, w_gs = (
        w_in0[:, offs[i]:offs[i + 1]].astype(BF16) for i in range(8))
    wts = (w_qd, w_kd, w_vd, w_qs, w_ks, w_vs, w_gd, w_gs,
           w_br_diff[0].astype(BF16), w_br_sb[0].astype(BF16), w_out[0].astype(BF16))
    lam4 = jnp.stack([lam_q1[0], lam_k1[0], lam_q2[0], lam_k2[0]]).astype(F32)
    g_norm = diff_norm_g[0].reshape(1, HEAD_W).astype(F32)

    c_all = jnp.concatenate([c_prompt, c_sample], axis=0)
    n_seq = c_all.shape[0]
    c_pad = jnp.pad(c_all, ((0, -n_seq % 8), (0, 0)))
    mod = _ada(c_pad, w_ada[0], b_ada[0])
    mod3 = mod.reshape(c_pad.shape[0] * 6, 1, d)

    xp = x_prompt.reshape(bp * sp, d)
    xs = x_sample.reshape(bs * ss, d)
    past = (jnp.transpose(cache_diff_k[0], (0, 2, 3, 4, 1)).reshape(bs, heads_d, HEAD_W, past_len),
            cache_diff_v[0].reshape(bs, past_len, diff_w),
            cache_sb_k[0].reshape(bs, past_len * heads_s, HEAD_W), cache_sb_v[0].reshape(bs, past_len * heads_s, HEAD_W))

    a_p, rows_p = _mixer(xp, mod3, 0, sp, None, wts, lam4, g_norm, lam_init)
    a_s, rows_s = _mixer(xs, mod3, bp, ss, past, wts, lam4, g_norm, lam_init)
    g1, b1 = ln1_g[0].reshape(1, d), ln1_b[0].reshape(1, d)
    x1_p, h2_p, h2f_p = _resid_norm(xp, a_p, mod3, 0, sp, g1, b1, alpha)
    x1_s, h2_s, h2f_s = _resid_norm(xs, a_s, mod3, bp, ss, g1, b1, alpha)

    n_p, n_s = bp * sp, bs * ss
    n_tok = n_p + n_s
    h2 = jnp.concatenate([h2_p, h2_s], axis=0)
    h2f = jnp.concatenate([h2f_p, h2f_s], axis=0)
    x1 = jnp.concatenate([x1_p, x1_s], axis=0)
    gate8, idx8, rank8, counts = _router(h2, w_router[0], router_bias[0])
    dest8, slot_tok, plan, n_used = _moe_plan(idx8[:, :TOP_K], rank8[:, :TOP_K], counts[0], n_tok, blk)
    act = _expert_up(h2f, w_e_gate[0], w_e_up[0], slot_tok, plan, n_used, blk)
    y_slots = _expert_down(act, w_e_down[0], plan, n_used, blk)
    sh_act = _glu(h2, w_sh_gate[0].astype(BF16), w_sh_up[0].astype(BF16))
    (shared,) = _proj(sh_act, w_sh_down[0].astype(BF16), (F32,))
    g2, b2 = ln2_g[0].reshape(1, d), ln2_b[0].reshape(1, d)
    y_p = _combine(y_slots, dest8, gate8, shared, x1, mod3, 0, sp, 0, n_p, g2, b2, alpha)
    y_s = _combine(y_slots, dest8, gate8, shared, x1, mod3, bp, ss, n_p, n_s, g2, b2, alpha)

    def rows(r, b, s):
        kd, vd, ks, vs = r
        return (kd.reshape(1, b, s, heads_d, 2, DIFF_DK), vd.reshape(1, b, s, heads_d, HEAD_W),
                ks.reshape(1, b, s, heads_s, HEAD_W), vs.reshape(1, b, s, heads_s, HEAD_W))

    return (y_p.reshape(bp, sp, d), y_s.reshape(bs, ss, d)) + rows(rows_p, bp, sp) + rows(rows_s, bs, ss)
```

```python
import functools

import jax
import jax.numpy as jnp
import numpy as np
from jax import lax
from jax.experimental import pallas as pl
from jax.experimental.pallas import tpu as pltpu

F32 = jnp.float32
BF16 = jnp.bfloat16

CHUNK = 64
DIFF_DK = 64
HEAD_W = 128
ROPE_THETA = 500000.0
ROPE_DIMS = DIFF_DK // 4
TOP_K = 8
ROUTE_SCALE = 2.5
LN_EPS = 1e-5
LANES = 128
VMEM_LIMIT = 52 * 1024 * 1024
SB_DEAD_LOG = -104.0
SAMPLE_PAST_CHUNK = 1024


def _cparams(sem):
    return pltpu.CompilerParams(dimension_semantics=sem, vmem_limit_bytes=VMEM_LIMIT)


def _tile(n, pref):
    if n <= pref:
        return n
    t = pref - pref % 64
    while n % t:
        t -= 64
    assert t > 0, (n, pref)
    return t


def _ln(x):
    xc = x - jnp.mean(x, axis=-1, keepdims=True)
    return xc * lax.rsqrt(jnp.mean(xc * xc, axis=-1, keepdims=True) + LN_EPS)


def _dot(a, b):
    return jnp.dot(a, b, preferred_element_type=F32)


def _dot_t(a, b):
    return lax.dot_general(a, b, (((1,), (1,)), ((), ())), preferred_element_type=F32)


def _ada_kernel(c_ref, w_ref, b_ref, o_ref):
    c = c_ref[...]
    s = (c * jax.nn.sigmoid(c)).astype(BF16)
    o_ref[...] = _dot(s, w_ref[...].astype(BF16)) + b_ref[...]


def _ada(c, w_ada, b_ada):
    m, d = c.shape
    n = w_ada.shape[1]
    tn = _tile(n, 512)
    return pl.pallas_call(
        _ada_kernel,
        grid=(n // tn,),
        in_specs=[pl.BlockSpec((m, d), lambda j: (0, 0)),
                  pl.BlockSpec((d, tn), lambda j: (0, j)),
                  pl.BlockSpec((1, tn), lambda j: (0, j))],
        out_specs=pl.BlockSpec((m, tn), lambda j: (0, j)),
        out_shape=jax.ShapeDtypeStruct((m, n), F32),
        compiler_params=_cparams(("arbitrary",)),
        name="ada",
    )(c, w_ada, b_ada.reshape(1, n))


def _ln_mod_kernel(x_ref, shift_ref, scale_ref, h_ref):
    h_ref[...] = (_ln(x_ref[...]) * (1.0 + scale_ref[...]) + shift_ref[...]).astype(h_ref.dtype)


def _mod_spec(d, comp, tiles_per_seq, row0=0):
    return pl.BlockSpec((None, 1, d), lambda i: ((row0 + i // tiles_per_seq) * 6 + comp, 0, 0))


def _ln_mod(x, mod3, row0, seq_len, shift_comp, scale_comp):
    t, d = x.shape
    tm = _tile(seq_len, 256)
    tps = seq_len // tm
    return pl.pallas_call(
        _ln_mod_kernel,
        grid=(t // tm,),
        in_specs=[pl.BlockSpec((tm, d), lambda i: (i, 0)),
                  _mod_spec(d, shift_comp, tps, row0),
                  _mod_spec(d, scale_comp, tps, row0)],
        out_specs=pl.BlockSpec((tm, d), lambda i: (i, 0)),
        out_shape=jax.ShapeDtypeStruct((t, d), BF16),
        compiler_params=_cparams(("parallel",)),
        name="ln_mod",
    )(x, mod3, mod3)


def _proj_kernel(*refs, rope, scale, n_out):
    x_ref, w_ref = refs[0], refs[1]
    outs = refs[len(refs) - n_out:]
    acc = _dot(x_ref[...], w_ref[...])
    if rope:
        c_ref, s1_ref, s2_ref = refs[2:5]
        cos, s1, s2 = c_ref[...], s1_ref[...], s2_ref[...]
        parts = []
        for g in range(acc.shape[1] // LANES):
            blk = acc[:, g * LANES:(g + 1) * LANES]
            half = ROPE_DIMS // 2
            parts.append(blk * cos + pltpu.roll(blk, LANES - half, 1) * s1 + pltpu.roll(blk, half, 1) * s2)
        acc = jnp.concatenate(parts, axis=1) if len(parts) > 1 else parts[0]
    if scale != 1.0:
        acc = acc * scale
    for o in outs:
        o[...] = acc.astype(o.dtype)


def _proj(x, w, out_dtypes, rope_tabs=None, scale=1.0, tm_pref=512, tn_pref=1024):
    m, k = x.shape
    n = w.shape[1]
    tm, tn = _tile(m, tm_pref), _tile(n, tn_pref)
    in_specs = [pl.BlockSpec((tm, k), lambda i, j: (i, 0)),
                pl.BlockSpec((k, tn), lambda i, j: (0, j))]
    args = [x, w]
    if rope_tabs is not None:
        in_specs += [pl.BlockSpec((tm, LANES), lambda i, j: (i, 0))] * 3
        args += list(rope_tabs)
    outs = pl.pallas_call(
        functools.partial(_proj_kernel, rope=rope_tabs is not None, scale=scale, n_out=len(out_dtypes)),
        grid=(m // tm, n // tn),
        in_specs=in_specs,
        out_specs=[pl.BlockSpec((tm, tn), lambda i, j: (i, j)) for _ in out_dtypes],
        out_shape=[jax.ShapeDtypeStruct((m, n), dt) for dt in out_dtypes],
        compiler_params=_cparams(("parallel", "arbitrary")),
        name="proj",
    )(*args)
    return outs


def _glu_kernel(x_ref, wg_ref, wu_ref, o_ref):
    x = x_ref[...]
    g = _dot(x, wg_ref[...])
    u = _dot(x, wu_ref[...])
    o_ref[...] = (g * jax.nn.sigmoid(g) * u).astype(o_ref.dtype)


def _glu(x, wg, wu):
    m, k = x.shape
    n = wg.shape[1]
    tm, tn = _tile(m, 512), _tile(n, 512)
    return pl.pallas_call(
        _glu_kernel,
        grid=(m // tm, n // tn),
        in_specs=[pl.BlockSpec((tm, k), lambda i, j: (i, 0)),
                  pl.BlockSpec((k, tn), lambda i, j: (0, j)),
                  pl.BlockSpec((k, tn), lambda i, j: (0, j))],
        out_specs=pl.BlockSpec((tm, tn), lambda i, j: (i, j)),
        out_shape=jax.ShapeDtypeStruct((m, n), BF16),
        compiler_params=_cparams(("parallel", "arbitrary")),
        name="shared_glu",
    )(x, wg, wu)


def _merge_kernel(od_ref, os_ref, wd_ref, ws_ref, gd_ref, gs_ref, o_ref):
    bd = _dot(od_ref[...], wd_ref[...])
    bs = _dot(os_ref[...], ws_ref[...])
    gd = jax.nn.sigmoid(gd_ref[...].astype(F32))
    gs = jax.nn.sigmoid(gs_ref[...].astype(F32))
    o_ref[...] = (gd * bd + gs * bs).astype(o_ref.dtype)


def _merge(od, osb, wbd, wbs, gd, gs):
    m, k = od.shape
    n = wbd.shape[1]
    tm, tn = _tile(m, 512), _tile(n, 1024)
    return pl.pallas_call(
        _merge_kernel,
        grid=(m // tm, n // tn),
        in_specs=[pl.BlockSpec((tm, k), lambda i, j: (i, 0)),
                  pl.BlockSpec((tm, k), lambda i, j: (i, 0)),
                  pl.BlockSpec((k, tn), lambda i, j: (0, j)),
                  pl.BlockSpec((k, tn), lambda i, j: (0, j)),
                  pl.BlockSpec((tm, tn), lambda i, j: (i, j)),
                  pl.BlockSpec((tm, tn), lambda i, j: (i, j))],
        out_specs=pl.BlockSpec((tm, tn), lambda i, j: (i, j)),
        out_shape=jax.ShapeDtypeStruct((m, n), BF16),
        compiler_params=_cparams(("parallel", "arbitrary")),
        name="merge",
    )(od, osb, wbd, wbs, gd, gs)


def _lambda(lam_ref, lam_init):
    a = jnp.sum(lam_ref[0:1, :] * lam_ref[1:2, :], axis=-1, keepdims=True)
    b = jnp.sum(lam_ref[2:3, :] * lam_ref[3:4, :], axis=-1, keepdims=True)
    return jnp.exp(a) - jnp.exp(b) + lam_init


def _split_components(q):
    lane = lax.broadcasted_iota(jnp.int32, q.shape, 1)
    zero = jnp.zeros_like(q)
    return jnp.where(lane < DIFF_DK, q, zero), jnp.where(lane >= DIFF_DK, q, zero)


def _diff_finish(o1, l1, o2, l2, lam, g, lam_init):
    o = o1 / l1 - lam * (o2 / l2)
    o = o * lax.rsqrt(jnp.mean(o * o, axis=-1, keepdims=True) + LN_EPS)
    return o * g * (1.0 - lam_init)


def _diff_prompt_kernel(lam_ref, g_ref, q_ref, k_ref, v_ref, o_ref, m_ref, l_ref, acc_ref, *, tq, tk, lam_init):
    i = pl.program_id(1)
    q1, q2 = _split_components(q_ref[...])
    m_ref[...] = jnp.full(m_ref.shape, -jnp.inf, F32)
    l_ref[...] = jnp.zeros(l_ref.shape, F32)
    acc_ref[...] = jnp.zeros(acc_ref.shape, F32)

    def block(start, keep):
        k = k_ref[pl.ds(start, tk), :]
        v = v_ref[pl.ds(start, tk), :]
        for comp, qc in enumerate((q1, q2)):
            s = _dot_t(qc, k)
            if keep is not None:
                s = jnp.where(keep[0], jnp.where(keep[1], s, -1e30), -1e30)
            m_old = m_ref[comp]
            m_new = jnp.maximum(m_old, jnp.max(s, axis=-1, keepdims=True))
            alpha = jnp.exp(m_old - m_new)
            p = jnp.exp(s - m_new)
            l_ref[comp] = alpha * l_ref[comp] + jnp.sum(p, axis=-1, keepdims=True)
            acc_ref[comp] = alpha * acc_ref[comp] + _dot(p.astype(BF16), v)
            m_ref[comp] = m_new

    n_full = (i * tq) // tk

    def body(j, carry):
        block(pl.multiple_of(j * tk, tk), None)
        return carry

    lax.fori_loop(0, n_full, body, 0)
    start = pl.multiple_of(jnp.maximum((i + 1) * tq - tk, 0), tq)
    lo = n_full * tk
    col = start + lax.broadcasted_iota(jnp.int32, (tq, tk), 1)
    row = i * tq + lax.broadcasted_iota(jnp.int32, (tq, 1), 0)
    hi = (row // CHUNK + 1) * CHUNK
    block(start, (col >= lo, col < hi))
    lam = _lambda(lam_ref, lam_init)
    o = _diff_finish(acc_ref[0], l_ref[0], acc_ref[1], l_ref[1], lam, g_ref[...], lam_init)
    o_ref[...] = o.astype(o_ref.dtype)


def _diff_prompt(q, k, v, lam4, g, lam_init):
    t, w = q.shape
    heads = w // HEAD_W
    tq = _tile(t, 256)
    tk = _tile(t, 1024)
    assert tq % CHUNK == 0 and tk % tq == 0
    return pl.pallas_call(
        functools.partial(_diff_prompt_kernel, tq=tq, tk=tk, lam_init=lam_init),
        grid=(heads, t // tq),
        in_specs=[pl.BlockSpec((4, DIFF_DK), lambda h, i: (0, 0)),
                  pl.BlockSpec((1, HEAD_W), lambda h, i: (0, 0)),
                  pl.BlockSpec((tq, HEAD_W), lambda h, i: (i, h)),
                  pl.BlockSpec((t, HEAD_W), lambda h, i: (0, h)),
                  pl.BlockSpec((t, HEAD_W), lambda h, i: (0, h))],
        out_specs=pl.BlockSpec((tq, HEAD_W), lambda h, i: (i, h)),
        out_shape=jax.ShapeDtypeStruct((t, w), BF16),
        scratch_shapes=[pltpu.VMEM((2, tq, 1), F32), pltpu.VMEM((2, tq, 1), F32),
                        pltpu.VMEM((2, tq, HEAD_W), F32)],
        compiler_params=_cparams(("parallel", "arbitrary")),
        name="diff_prompt",
    )(lam4, g, q, k, v)


def _diff_sample_kernel(lam_ref, g_ref, q_ref, kn_ref, vn_ref, kct_ref, vc_ref, o_ref, *, past_len, lam_init):
    q1, q2 = _split_components(q_ref[...])
    kpt = kct_ref[...].astype(BF16)
    vp = vc_ref[...].astype(BF16)
    kn = kn_ref[...].astype(BF16)
    vn = vn_ref[...].astype(BF16)
    tn = kn.shape[0]
    r = (past_len + lax.broadcasted_iota(jnp.int32, (tn, tn), 0)) // CHUNK
    c = (past_len + lax.broadcasted_iota(jnp.int32, (tn, tn), 1)) // CHUNK
    keep = c <= r
    res = []
    for qc in (q1, q2):
        sp = _dot(qc, kpt)
        sn = jnp.where(keep, _dot_t(qc, kn), -1e30)
        m = jnp.maximum(jnp.max(sp, axis=-1, keepdims=True), jnp.max(sn, axis=-1, keepdims=True))
        pp = jnp.exp(sp - m)
        pn = jnp.exp(sn - m)
        l = jnp.sum(pp, axis=-1, keepdims=True) + jnp.sum(pn, axis=-1, keepdims=True)
        res.append((_dot(pp.astype(BF16), vp) + _dot(pn.astype(BF16), vn), l))
    lam = _lambda(lam_ref, lam_init)
    o = _diff_finish(res[0][0], res[0][1], res[1][0], res[1][1], lam, g_ref[...], lam_init)
    o_ref[...] = o.astype(o_ref.dtype)


def _diff_sample(q, kn, vn, kct, vc, lam4, g, lam_init, seq_len):
    t, w = q.shape
    b, heads, _, past_len = kct.shape
    assert past_len % CHUNK == 0
    row = pl.BlockSpec((seq_len, HEAD_W), lambda bi, h: (bi, h))
    cache = pl.BlockSpec((None, past_len, HEAD_W), lambda bi, h: (bi, 0, h))
    cache_t = pl.BlockSpec((None, None, HEAD_W, past_len), lambda bi, h: (bi, h, 0, 0))
    return pl.pallas_call(
        functools.partial(_diff_sample_kernel, past_len=past_len, lam_init=lam_init),
        grid=(b, heads),
        in_specs=[pl.BlockSpec((4, DIFF_DK), lambda bi, h: (0, 0)),
                  pl.BlockSpec((1, HEAD_W), lambda bi, h: (0, 0)),
                  row, row, row, cache_t, cache],
        out_specs=row,
        out_shape=jax.ShapeDtypeStruct((t, w), BF16),
        compiler_params=_cparams(("parallel", "arbitrary")),
        name="diff_sample",
    )(lam4, g, q, kn, vn, kct, vc)


def _head_rows(cache_ref, head, start, size, heads):
    return cache_ref[pl.ds(start * heads + head, size, stride=heads), :]


def _head_major(x, b, s):
    return x.reshape(b, s, -1, HEAD_W).transpose(0, 2, 1, 3)


def _token_major(x):
    b, h, s, w = x.shape
    return x.transpose(0, 2, 1, 3).reshape(b * s, h * w)


def _sample_specs(seq_len, heads, pc, chunk_index):
    row = pl.BlockSpec((None, heads, seq_len, HEAD_W), lambda bi, c: (bi, 0, 0, 0))
    cache = pl.BlockSpec((None, pc * heads, HEAD_W), lambda bi, c: (bi, chunk_index(c), 0))
    return row, cache


def _sb_block(q, k, v, carry_ref, acc_ref, keep):
    tk = k.shape[0]
    z = _dot_t(q, k)
    soft = jnp.log(1.0 + jnp.exp(-jnp.abs(z)))
    log_sig = jnp.minimum(z, 0.0) - soft
    log_keep = jnp.minimum(-z, 0.0) - soft
    if keep is not None:
        log_keep = jnp.where(keep, log_keep, 0.0)
    upper = (lax.broadcasted_iota(jnp.int32, (tk, tk), 0) > lax.broadcasted_iota(jnp.int32, (tk, tk), 1))
    upper = jnp.where(upper, 1.0, 0.0).astype(BF16)
    hi = log_keep.astype(BF16)
    lo = (log_keep - hi.astype(F32)).astype(BF16)
    later = _dot(hi, upper) + _dot(lo, upper) + carry_ref[...]
    w = jnp.exp(log_sig + later)
    if keep is not None:
        w = jnp.where(keep, w, 0.0)
    acc_ref[...] += _dot(w.astype(BF16), v)
    carry_ref[...] += jnp.sum(log_keep, axis=-1, keepdims=True)


def _sb_prompt_kernel(q_ref, k_ref, v_ref, o_ref, carry_ref, acc_ref, *, tq):
    i = pl.program_id(1)
    q = q_ref[...]
    carry_ref[...] = jnp.zeros(carry_ref.shape, F32)
    acc_ref[...] = jnp.zeros(acc_ref.shape, F32)

    def load(j):
        start = pl.multiple_of(j * tq, tq)
        return k_ref[pl.ds(start, tq), :], v_ref[pl.ds(start, tq), :]

    strict = (lax.broadcasted_iota(jnp.int32, (tq, tq), 1) < lax.broadcasted_iota(jnp.int32, (tq, tq), 0))
    k, v = load(i)
    _sb_block(q, k, v, carry_ref, acc_ref, strict)

    def cond(state):
        j, alive = state
        return jnp.logical_and(j >= 0, alive > SB_DEAD_LOG)

    def body(state):
        j, _ = state
        kj, vj = load(j)
        _sb_block(q, kj, vj, carry_ref, acc_ref, None)
        return j - 1, jnp.max(carry_ref[...])

    lax.while_loop(cond, body, (i - 1, jnp.max(carry_ref[...])))
    o_ref[...] = acc_ref[...].astype(o_ref.dtype)


def _sb_prompt(q, k, v):
    t, w = q.shape
    heads = w // HEAD_W
    tq = _tile(t, 256)
    return pl.pallas_call(
        functools.partial(_sb_prompt_kernel, tq=tq),
        grid=(heads, t // tq),
        in_specs=[pl.BlockSpec((tq, HEAD_W), lambda h, i: (i, h)),
                  pl.BlockSpec((t, HEAD_W), lambda h, i: (0, h)),
                  pl.BlockSpec((t, HEAD_W), lambda h, i: (0, h))],
        out_specs=pl.BlockSpec((tq, HEAD_W), lambda h, i: (i, h)),
        out_shape=jax.ShapeDtypeStruct((t, w), BF16),
        scratch_shapes=[pltpu.VMEM((tq, 1), F32), pltpu.VMEM((tq, HEAD_W), F32)],
        compiler_params=_cparams(("parallel", "arbitrary")),
        name="sb_prompt",
    )(q, k, v)


def _sb_sample_kernel(q_ref, kn_ref, vn_ref, kc_ref, vc_ref, o_ref, carry_ref, acc_ref, *, heads, pc, tk):
    c = pl.program_id(1)
    tn = q_ref.shape[1]

    @pl.when(c == 0)
    def _():
        carry_ref[...] = jnp.zeros(carry_ref.shape, F32)
        acc_ref[...] = jnp.zeros(acc_ref.shape, F32)
        strict = (lax.broadcasted_iota(jnp.int32, (tn, tn), 1) < lax.broadcasted_iota(jnp.int32, (tn, tn), 0))

        def new_rows(h, carry):
            _sb_block(q_ref[h], kn_ref[h].astype(BF16), vn_ref[h].astype(BF16), carry_ref.at[h], acc_ref.at[h], strict)
            return carry

        lax.fori_loop(0, heads, new_rows, 0)

    for j in reversed(range(pc // tk)):
        @pl.when(jnp.max(carry_ref[...]) > SB_DEAD_LOG)
        def _():
            def past_rows(h, carry):
                kj = _head_rows(kc_ref, h, j * tk, tk, heads).astype(BF16)
                vj = _head_rows(vc_ref, h, j * tk, tk, heads).astype(BF16)
                _sb_block(q_ref[h], kj, vj, carry_ref.at[h], acc_ref.at[h], None)
                return carry

            lax.fori_loop(0, heads, past_rows, 0)

    @pl.when(c == pl.num_programs(1) - 1)
    def _():
        o_ref[...] = acc_ref[...].astype(o_ref.dtype)


def _sb_sample(q, kn, vn, kc, vc, seq_len):
    t, w = q.shape
    heads = w // HEAD_W
    b = kc.shape[0]
    past_len = kc.shape[1] // heads
    pc = _tile(past_len, SAMPLE_PAST_CHUNK)
    n_c = past_len // pc
    tk = _tile(pc, 256)
    row, cache = _sample_specs(seq_len, heads, pc, lambda c: n_c - 1 - c)
    out = pl.pallas_call(
        functools.partial(_sb_sample_kernel, heads=heads, pc=pc, tk=tk),
        grid=(b, n_c),
        in_specs=[row, row, row, cache, cache],
        out_specs=row,
        out_shape=jax.ShapeDtypeStruct((b, heads, seq_len, HEAD_W), BF16),
        scratch_shapes=[pltpu.VMEM((heads, seq_len, 1), F32), pltpu.VMEM((heads, seq_len, HEAD_W), F32)],
        compiler_params=_cparams(("parallel", "arbitrary")),
        name="sb_sample",
    )(_head_major(q, b, seq_len), _head_major(kn, b, seq_len), _head_major(vn, b, seq_len), kc, vc)
    return _token_major(out)


def _pack_halves(x):
    n = x.shape[1] // 2
    bits = pltpu.bitcast(x.astype(BF16).astype(F32), jnp.uint32)
    return bits[:, n:] | (bits[:, :n] >> 16)


def _unpack_halves(u):
    lo = pltpu.bitcast(u << 16, F32)
    hi = pltpu.bitcast(u & jnp.uint32(0xFFFF0000), F32)
    return lo, hi


def _resid_norm_kernel(x_ref, a_ref, gate_ref, g_ref, b_ref, shift_ref, scale_ref, x1_ref, h_ref, hf_ref, *, alpha):
    x1 = _ln(alpha * x_ref[...] + gate_ref[...] * a_ref[...]) * g_ref[...] + b_ref[...]
    x1_ref[...] = x1
    h = _ln(x1) * (1.0 + scale_ref[...]) + shift_ref[...]
    h_ref[...] = h.astype(h_ref.dtype)
    hf_ref[...] = h


def _resid_norm(x, a, mod3, row0, seq_len, g, b, alpha):
    t, d = x.shape
    tm = _tile(seq_len, 256)
    tps = seq_len // tm
    tok = pl.BlockSpec((tm, d), lambda i: (i, 0))
    vec = pl.BlockSpec((1, d), lambda i: (0, 0))
    return pl.pallas_call(
        functools.partial(_resid_norm_kernel, alpha=alpha),
        grid=(t // tm,),
        in_specs=[tok, tok, _mod_spec(d, 2, tps, row0), vec, vec,
                  _mod_spec(d, 3, tps, row0), _mod_spec(d, 4, tps, row0)],
        out_specs=[tok, tok, tok],
        out_shape=[jax.ShapeDtypeStruct((t, d), F32), jax.ShapeDtypeStruct((t, d), BF16),
                   jax.ShapeDtypeStruct((t, d), F32)],
        compiler_params=_cparams(("parallel",)),
        name="resid_norm",
    )(x, a, mod3, g, b, mod3, mod3)


def _router_kernel(h_ref, whi_ref, wlo_ref, bias_ref, gate_ref, idx_ref, rank_ref, cnt_ref, run_ref):
    @pl.when(pl.program_id(0) == 0)
    def _():
        run_ref[...] = jnp.zeros(run_ref.shape, F32)

    h = h_ref[...]
    logits = _dot(h, whi_ref[...]) + _dot(h, wlo_ref[...])
    scores = jax.nn.sigmoid(logits)
    tm, e = scores.shape
    lane = lax.broadcasted_iota(jnp.int32, (tm, e), 1)
    sel = scores + bias_ref[...]
    picked = jnp.zeros((tm, e), F32)
    idx = jnp.zeros((tm, e), jnp.int32)
    firsts = []
    for k in range(TOP_K):
        best = jnp.max(sel, axis=-1, keepdims=True)
        first = jnp.min(jnp.where(sel == best, lane, e), axis=-1, keepdims=True)
        hit = lane == first
        picked = jnp.where(hit, 1.0, picked)
        sel = jnp.where(hit, -jnp.inf, sel)
        idx = jnp.where(lane == k, first, idx)
        firsts.append(first)
    gate = scores * picked
    gate = gate / jnp.sum(gate, axis=-1, keepdims=True) * ROUTE_SCALE
    lower = (lax.broadcasted_iota(jnp.int32, (tm, tm), 1) < lax.broadcasted_iota(jnp.int32, (tm, tm), 0))
    within = _dot(jnp.where(lower, 1.0, 0.0).astype(BF16), picked.astype(BF16)) + run_ref[...]
    gate8 = jnp.zeros((tm, e), F32)
    rank8 = jnp.zeros((tm, e), F32)
    for k, first in enumerate(firsts):
        hit = lane == first
        gk = jnp.sum(jnp.where(hit, gate, 0.0), axis=-1, keepdims=True)
        rk = jnp.sum(jnp.where(hit, within, 0.0), axis=-1, keepdims=True)
        gate8 = jnp.where(lane == k, gk, gate8)
        rank8 = jnp.where(lane == k, rk, rank8)
    gate_ref[...] = gate8
    idx_ref[...] = idx
    rank_ref[...] = rank8.astype(jnp.int32)
    run_ref[...] += jnp.sum(picked, axis=0, keepdims=True)
    cnt_ref[...] = run_ref[...].astype(jnp.int32)


def _router(h, w_router, router_bias):
    t, d = h.shape
    e = w_router.shape[1]
    assert e == LANES
    tm = _tile(t, 512)
    whi = w_router.astype(BF16)
    wlo = (w_router - whi.astype(F32)).astype(BF16)
    tok = pl.BlockSpec((tm, e), lambda i: (i, 0))
    wspec = pl.BlockSpec((d, e), lambda i: (0, 0))
    one = pl.BlockSpec((1, e), lambda i: (0, 0))
    return pl.pallas_call(
        _router_kernel,
        grid=(t // tm,),
        in_specs=[pl.BlockSpec((tm, d), lambda i: (i, 0)), wspec, wspec, one],
        out_specs=[tok, tok, tok, one],
        out_shape=[jax.ShapeDtypeStruct((t, e), F32), jax.ShapeDtypeStruct((t, e), jnp.int32),
                   jax.ShapeDtypeStruct((t, e), jnp.int32), jax.ShapeDtypeStruct((1, e), jnp.int32)],
        scratch_shapes=[pltpu.VMEM((1, e), F32)],
        compiler_params=_cparams(("arbitrary",)),
        name="router",
    )(h, whi, wlo, router_bias.reshape(1, e))


def _row_copy(src_hbm, tok, buf, slot, r, sem):
    return pltpu.make_async_copy(src_hbm.at[pl.ds(tok, 1)], buf.at[slot, pl.ds(r, 1)], sem.at[slot])


def _expert_weights(plan_ref, b, w_hbms, w_bufs, w_sem):
    expert, first, slot = plan_ref[0, b], plan_ref[1, b], plan_ref[2, b]
    has_next, nxt = plan_ref[3, b], plan_ref[4, b]

    def copies(e, into):
        return [pltpu.make_async_copy(w.at[e], buf.at[into], w_sem.at[into, n])
                for n, (w, buf) in enumerate(zip(w_hbms, w_bufs))]

    @pl.when(b == 0)
    def _():
        for cp in copies(expert, slot):
            cp.start(priority=1)

    @pl.when(first == 1)
    def _():
        for cp in copies(expert, slot):
            cp.wait()

    @pl.when(jnp.logical_and(first == 1, has_next == 1))
    def _():
        for cp in copies(nxt, 1 - slot):
            cp.start(priority=1)

    return slot


def _expert_up_kernel(plan_ref, nused_ref, tok_ref, tok_next_ref, h_hbm, wg_hbm, wu_hbm, o_ref, buf, sem,
                      wg_buf, wu_buf, w_sem, *, blk, n_blocks):
    b = pl.program_id(0)
    n_used = nused_ref[0]
    slot = lax.rem(b, 2)

    def start_rows(toks, into):
        for r in range(blk):
            _row_copy(h_hbm, toks[0, r], buf, into, r, sem).start()

    def wait_rows(into):
        for r in range(blk):
            _row_copy(h_hbm, 0, buf, into, r, sem).wait()

    @pl.when(b == 0)
    def _():
        def body(r, carry):
            _row_copy(h_hbm, tok_ref[0, r], buf, slot, r, sem).start()
            return carry
        lax.fori_loop(0, blk, body, 0)

    @pl.when(b < n_used)
    def _():
        ws = _expert_weights(plan_ref, b, (wg_hbm, wu_hbm), (wg_buf, wu_buf), w_sem)
        wait_rows(slot)
        start_rows(tok_next_ref, 1 - slot)
        x = buf[slot].astype(BF16)
        g = _dot(x, wg_buf[ws].astype(BF16))
        u = _dot(x, wu_buf[ws].astype(BF16))
        o_ref[...] = (g * jax.nn.sigmoid(g) * u).astype(o_ref.dtype)

        @pl.when(b == n_blocks - 1)
        def _():
            wait_rows(1 - slot)

    @pl.when(b == n_used)
    def _():
        wait_rows(slot)

    @pl.when(b >= n_used)
    def _():
        o_ref[...] = jnp.zeros(o_ref.shape, o_ref.dtype)


def _expert_up(h, wg, wu, slot_tok, plan, n_used, blk):
    e, d, hid = wg.shape
    n_blocks = plan.shape[1]
    tok3 = slot_tok.reshape(n_blocks, 1, blk)
    grid_spec = pltpu.PrefetchScalarGridSpec(
        num_scalar_prefetch=2,
        grid=(n_blocks,),
        in_specs=[pl.BlockSpec((None, 1, blk), lambda b, pr, nu: (b, 0, 0), memory_space=pltpu.SMEM),
                  pl.BlockSpec((None, 1, blk), lambda b, pr, nu: (jnp.minimum(b + 1, n_blocks - 1), 0, 0),
                               memory_space=pltpu.SMEM),
                  pl.BlockSpec(memory_space=pl.ANY), pl.BlockSpec(memory_space=pl.ANY),
                  pl.BlockSpec(memory_space=pl.ANY)],
        out_specs=pl.BlockSpec((blk, hid), lambda b, pr, nu: (b, 0)),
        scratch_shapes=[pltpu.VMEM((2, blk, d), F32), pltpu.SemaphoreType.DMA((2,)),
                        pltpu.VMEM((2, d, hid), F32), pltpu.VMEM((2, d, hid), F32),
                        pltpu.SemaphoreType.DMA((2, 2))],
    )
    return pl.pallas_call(
        functools.partial(_expert_up_kernel, blk=blk, n_blocks=n_blocks),
        grid_spec=grid_spec,
        out_shape=jax.ShapeDtypeStruct((n_blocks * blk, hid), BF16),
        compiler_params=_cparams(("arbitrary",)),
        name="expert_up",
    )(plan, n_used, tok3, tok3, h, wg, wu)


def _expert_down_kernel(plan_ref, nused_ref, a_ref, wd_hbm, o_ref, wd_buf, w_sem):
    b = pl.program_id(0)

    @pl.when(b < nused_ref[0])
    def _():
        ws = _expert_weights(plan_ref, b, (wd_hbm,), (wd_buf,), w_sem)
        o_ref[...] = _pack_halves(_dot(a_ref[...], wd_buf[ws].astype(BF16)))

    @pl.when(b >= nused_ref[0])
    def _():
        o_ref[...] = jnp.zeros(o_ref.shape, o_ref.dtype)


def _expert_down(act, wd, plan, n_used, blk):
    e, hid, d = wd.shape
    n_blocks = plan.shape[1]
    grid_spec = pltpu.PrefetchScalarGridSpec(
        num_scalar_prefetch=2,
        grid=(n_blocks,),
        in_specs=[pl.BlockSpec((blk, hid), lambda b, pr, nu: (b, 0)),
                  pl.BlockSpec(memory_space=pl.ANY)],
        out_specs=pl.BlockSpec((blk, d // 2), lambda b, pr, nu: (b, 0)),
        scratch_shapes=[pltpu.VMEM((2, hid, d), F32), pltpu.SemaphoreType.DMA((2, 1))],
    )
    return pl.pallas_call(
        _expert_down_kernel,
        grid_spec=grid_spec,
        out_shape=jax.ShapeDtypeStruct((n_blocks * blk, d // 2), jnp.uint32),
        compiler_params=_cparams(("arbitrary",)),
        name="expert_down",
    )(plan, n_used, act, wd)


def _combine_kernel(dest_ref, dest_next_ref, y_hbm, gate_ref, sh_ref, x_ref, gf_ref, g_ref, b_ref, o_ref, buf, sem, *,
                    tt, n_steps, alpha):
    i = pl.program_id(0)
    slot = lax.rem(i, 2)

    def copy(src_row, into, r, k):
        return pltpu.make_async_copy(y_hbm.at[pl.ds(src_row, 1)], buf.at[into, k, pl.ds(r, 1)], sem.at[into])

    def start_rows(dref, into):
        for r in range(tt):
            for k in range(TOP_K):
                copy(dref[0, r * TOP_K + k], into, r, k).start()

    def wait_rows(into):
        for r in range(tt):
            for k in range(TOP_K):
                copy(0, into, r, k).wait()

    @pl.when(i == 0)
    def _():
        def body(r, carry):
            for k in range(TOP_K):
                copy(dest_ref[0, r * TOP_K + k], slot, r, k).start()
            return carry
        lax.fori_loop(0, tt, body, 0)

    wait_rows(slot)
    start_rows(dest_next_ref, 1 - slot)
    gate = gate_ref[...]
    n = buf.shape[-1]
    f_lo, f_hi = sh_ref[:, :n], sh_ref[:, n:]
    for k in range(TOP_K):
        y_lo, y_hi = _unpack_halves(buf[slot, k])
        f_lo = f_lo + y_lo * gate[:, k:k + 1]
        f_hi = f_hi + y_hi * gate[:, k:k + 1]
    f = jnp.concatenate([f_lo, f_hi], axis=1)
    o_ref[...] = _ln(alpha * x_ref[...] + gf_ref[...] * f) * g_ref[...] + b_ref[...]

    @pl.when(i == n_steps - 1)
    def _():
        wait_rows(1 - slot)


def _combine(y_slots, dest8, gate8, shared, x1, mod3, row0, seq_len, tok0, n_tok, g, b, alpha):
    d = x1.shape[1]
    tt = _tile(seq_len, 64)
    assert tok0 % tt == 0
    tile0 = tok0 // tt
    tps = seq_len // tt
    n_steps = n_tok // tt
    tok = pl.BlockSpec((tt, d), lambda i: (tile0 + i, 0))
    vec = pl.BlockSpec((1, d), lambda i: (0, 0))
    dest3 = dest8.reshape(-1, 1, tt * TOP_K)
    return pl.pallas_call(
        functools.partial(_combine_kernel, tt=tt, n_steps=n_steps, alpha=alpha),
        grid=(n_steps,),
        in_specs=[pl.BlockSpec((None, 1, tt * TOP_K), lambda i: (tile0 + i, 0, 0), memory_space=pltpu.SMEM),
                  pl.BlockSpec((None, 1, tt * TOP_K), lambda i: (tile0 + jnp.minimum(i + 1, n_steps - 1), 0, 0),
                               memory_space=pltpu.SMEM),
                  pl.BlockSpec(memory_space=pl.ANY),
                  pl.BlockSpec((tt, LANES), lambda i: (tile0 + i, 0)),
                  tok, tok,
                  pl.BlockSpec((None, 1, d), lambda i: ((row0 + i // tps) * 6 + 5, 0, 0)),
                  vec, vec],
        out_specs=pl.BlockSpec((tt, d), lambda i: (i, 0)),
        out_shape=jax.ShapeDtypeStruct((n_tok, d), F32),
        scratch_shapes=[pltpu.VMEM((2, TOP_K, tt, d // 2), jnp.uint32), pltpu.SemaphoreType.DMA((2,))],
        compiler_params=_cparams(("arbitrary",)),
        name="combine",
    )(dest3, dest3, y_slots, gate8, shared, x1, mod3, g, b)


def _rope_tables(pos):
    half = ROPE_DIMS // 2
    inv_freq = ROPE_THETA ** (-jnp.arange(half, dtype=F32) / half)
    ang = pos.astype(F32)[:, None] * inv_freq[None, :]
    cos, sin = jnp.cos(ang), jnp.sin(ang)
    n = pos.shape[0]
    ones = jnp.ones((n, DIFF_DK - ROPE_DIMS), F32)
    zeros = jnp.zeros((n, DIFF_DK - ROPE_DIMS), F32)
    zh = jnp.zeros((n, half), F32)
    c64 = jnp.concatenate([cos, cos, ones], axis=1)
    s1 = jnp.concatenate([-sin, zh, zeros], axis=1)
    s2 = jnp.concatenate([zh, sin, zeros], axis=1)
    rep = LANES // DIFF_DK
    return tuple(jnp.tile(a, (1, rep)) for a in (c64, s1, s2))


def _mixer(x, mod3, row0, seq_len, past, wts, lam4, g_norm, lam_init):
    (w_qd, w_kd, w_vd, w_qs, w_ks, w_vs, w_gd, w_gs, w_bd, w_bs, w_out) = wts
    t = x.shape[0]
    past_len = 0 if past is None else past[0].shape[-1]
    pos = past_len + jnp.arange(seq_len, dtype=jnp.int32)
    tabs = _rope_tables(jnp.tile(pos, t // seq_len))
    h = _ln_mod(x, mod3, row0, seq_len, 0, 1)
    (qd,) = _proj(h, w_qd, (BF16,), rope_tabs=tabs, scale=DIFF_DK ** -0.5)
    (qs,) = _proj(h, w_qs, (BF16,), scale=HEAD_W ** -0.5)
    (gd,) = _proj(h, w_gd, (BF16,))
    (gs,) = _proj(h, w_gs, (BF16,))
    if past is None:
        kd, kd_b = _proj(h, w_kd, (F32, BF16), rope_tabs=tabs)
        vd, vd_b = _proj(h, w_vd, (F32, BF16))
        ks, ks_b = _proj(h, w_ks, (F32, BF16))
        vs, vs_b = _proj(h, w_vs, (F32, BF16))
        od = _diff_prompt(qd, kd_b, vd_b, lam4, g_norm, lam_init)
        osb = _sb_prompt(qs, ks_b, vs_b)
    else:
        (kd,) = _proj(h, w_kd, (F32,), rope_tabs=tabs)
        (vd,) = _proj(h, w_vd, (F32,))
        (ks,) = _proj(h, w_ks, (F32,))
        (vs,) = _proj(h, w_vs, (F32,))
        od = _diff_sample(qd, kd, vd, past[0], past[1], lam4, g_norm, lam_init, seq_len)
        osb = _sb_sample(qs, ks, vs, past[2], past[3], seq_len)
    merged = _merge(od, osb, w_bd, w_bs, gd, gs)
    (a,) = _proj(merged, w_out, (F32,))
    return a, (kd, vd, ks, vs)


def _moe_plan(idx8, rank8, counts, n_tok, blk):
    e = counts.shape[0]
    padded = (counts + blk - 1) // blk * blk
    pad_ends = jnp.cumsum(padded)
    pad_starts = pad_ends - padded
    hit = idx8[..., None] == jnp.arange(e, dtype=jnp.int32)
    dest8 = (jnp.sum(jnp.where(hit, pad_starts, 0), axis=-1) + rank8).astype(jnp.int32)
    n_blocks = -(-n_tok * TOP_K // blk) + e
    tok_ids = jnp.repeat(jnp.arange(n_tok, dtype=jnp.int32), TOP_K)
    slot_tok = jnp.zeros((n_blocks * blk,), jnp.int32).at[dest8.reshape(-1)].set(tok_ids)
    block_pos = jnp.arange(n_blocks, dtype=jnp.int32) * blk
    block_expert = jnp.minimum(jnp.searchsorted(pad_ends, block_pos, side='right'), e - 1).astype(jnp.int32)
    n_used = (pad_ends[-1] // blk).astype(jnp.int32).reshape(1)
    bidx = jnp.arange(n_blocks, dtype=jnp.int32)
    first = jnp.concatenate([jnp.ones((1,), bool), block_expert[1:] != block_expert[:-1]]) & (bidx < n_used[0])
    wslot = (jnp.cumsum(first.astype(jnp.int32)) - 1) % 2
    first_pos = jnp.where(first, bidx, n_blocks)
    later_first = jnp.concatenate([lax.cummin(first_pos, reverse=True)[1:], jnp.full((1,), n_blocks, jnp.int32)])
    has_next = later_first < n_blocks
    next_expert = block_expert[jnp.minimum(later_first, n_blocks - 1)]
    plan = jnp.stack([block_expert, first.astype(jnp.int32), wslot.astype(jnp.int32), has_next.astype(jnp.int32),
                      next_expert]).astype(jnp.int32)
    return dest8, slot_tok, plan, n_used


def kernel(x_prompt, x_sample, c_prompt, c_sample, cache_diff_k, cache_diff_v, cache_sb_k, cache_sb_v, w_ada, b_ada, w_in, w_br_diff, w_br_sb, w_out, lam_q1, lam_k1, lam_q2, lam_k2, diff_norm_g, ln1_g, ln1_b, w_router, router_bias, w_e_gate, w_e_up, w_e_down, w_sh_gate, w_sh_up, w_sh_down, ln2_g, ln2_b):
    depth = w_ada.shape[0]
    assert depth == 1
    bp, sp, d = x_prompt.shape
    bs, ss, _ = x_sample.shape
    assert bp == 1
    diff_w = w_br_diff.shape[1]
    sb_w = w_br_sb.shape[1]
    heads_d, heads_s = diff_w // HEAD_W, sb_w // HEAD_W
    past_len = cache_diff_k.shape[2]
    alpha = (2.0 * depth) ** 0.25
    lam_init = 0.8 - 0.6 * float(np.exp(-0.3 * 0))
    blk = 256

    widths = [diff_w, diff_w, diff_w, sb_w, sb_w, sb_w, d, d]
    offs = np.concatenate([[0], np.cumsum(widths)])
    w_in0 = w_in[0]
    w_qd, w_kd, w_vd, w_qs, w_ks, w_vs, w_gd, w_gs = (
        w_in0[:, offs[i]:offs[i + 1]].astype(BF16) for i in range(8))
    wts = (w_qd, w_kd, w_vd, w_qs, w_ks, w_vs, w_gd, w_gs,
           w_br_diff[0].astype(BF16), w_br_sb[0].astype(BF16), w_out[0].astype(BF16))
    lam4 = jnp.stack([lam_q1[0], lam_k1[0], lam_q2[0], lam_k2[0]]).astype(F32)
    g_norm = diff_norm_g[0].reshape(1, HEAD_W).astype(F32)

    c_all = jnp.concatenate([c_prompt, c_sample], axis=0)
    n_seq = c_all.shape[0]
    c_pad = jnp.pad(c_all, ((0, -n_seq % 8), (0, 0)))
    mod = _ada(c_pad, w_ada[0], b_ada[0])
    mod3 = mod.reshape(c_pad.shape[0] * 6, 1, d)

    xp = x_prompt.reshape(bp * sp, d)
    xs = x_sample.reshape(bs * ss, d)
    past = (jnp.transpose(cache_diff_k[0], (0, 2, 3, 4, 1)).reshape(bs, heads_d, HEAD_W, past_len),
            cache_diff_v[0].reshape(bs, past_len, diff_w),
            cache_sb_k[0].reshape(bs, past_len * heads_s, HEAD_W), cache_sb_v[0].reshape(bs, past_len * heads_s, HEAD_W))

    a_p, rows_p = _mixer(xp, mod3, 0, sp, None, wts, lam4, g_norm, lam_init)
    a_s, rows_s = _mixer(xs, mod3, bp, ss, past, wts, lam4, g_norm, lam_init)
    g1, b1 = ln1_g[0].reshape(1, d), ln1_b[0].reshape(1, d)
    x1_p, h2_p, h2f_p = _resid_norm(xp, a_p, mod3, 0, sp, g1, b1, alpha)
    x1_s, h2_s, h2f_s = _resid_norm(xs, a_s, mod3, bp, ss, g1, b1, alpha)

    n_p, n_s = bp * sp, bs * ss
    n_tok = n_p + n_s
    h2 = jnp.concatenate([h2_p, h2_s], axis=0)
    h2f = jnp.concatenate([h2f_p, h2f_s], axis=0)
    x1 = jnp.concatenate([x1_p, x1_s], axis=0)
    gate8, idx8, rank8, counts = _router(h2, w_router[0], router_bias[0])
    dest8, slot_tok, plan, n_used = _moe_plan(idx8[:, :TOP_K], rank8[:, :TOP_K], counts[0], n_tok, blk)
    act = _expert_up(h2f, w_e_gate[0], w_e_up[0], slot_tok, plan, n_used, blk)
    y_slots = _expert_down(act, w_e_down[0], plan, n_used, blk)
    sh_act = _glu(h2, w_sh_gate[0].astype(BF16), w_sh_up[0].astype(BF16))
    (shared,) = _proj(sh_act, w_sh_down[0].astype(BF16), (F32,))
    g2, b2 = ln2_g[0].reshape(1, d), ln2_b[0].reshape(1, d)
    y_p = _combine(y_slots, dest8, gate8, shared, x1, mod3, 0, sp, 0, n_p, g2, b2, alpha)
    y_s = _combine(y_slots, dest8, gate8, shared, x1, mod3, bp, ss, n_p, n_s, g2, b2, alpha)

    def rows(r, b, s):
        kd, vd, ks, vs = r
        return (kd.reshape(1, b, s, heads_d, 2, DIFF_DK), vd.reshape(1, b, s, heads_d, HEAD_W),
                ks.reshape(1, b, s, heads_s, HEAD_W), vs.reshape(1, b, s, heads_s, HEAD_W))

    return (y_p.reshape(bp, sp, d), y_s.reshape(bs, ss, d)) + rows(rows_p, bp, sp) + rows(rows_s, bs, ss)
```

```python
import functools

import jax
import jax.numpy as jnp
import numpy as np
from jax import lax
from jax.experimental import pallas as pl
from jax.experimental.pallas import tpu as pltpu

F32 = jnp.float32
BF16 = jnp.bfloat16

CHUNK = 64
DIFF_DK = 64
HEAD_W = 128
ROPE_THETA = 500000.0
ROPE_DIMS = DIFF_DK // 4
TOP_K = 8
ROUTE_SCALE = 2.5
LN_EPS = 1e-5
LANES = 128
VMEM_LIMIT = 52 * 1024 * 1024
SB_DEAD_LOG = -104.0
GATHER_BATCHES = 4
SAMPLE_PAST_CHUNK = 1024


def _cparams(sem):
    return pltpu.CompilerParams(dimension_semantics=sem, vmem_limit_bytes=VMEM_LIMIT)


def _tile(n, pref):
    if n <= pref:
        return n
    t = pref - pref % 64
    while n % t:
        t -= 64
    assert t > 0, (n, pref)
    return t


def _ln(x):
    xc = x - jnp.mean(x, axis=-1, keepdims=True)
    return xc * lax.rsqrt(jnp.mean(xc * xc, axis=-1, keepdims=True) + LN_EPS)


def _dot(a, b):
    return jnp.dot(a, b, preferred_element_type=F32)


def _dot_t(a, b):
    return lax.dot_general(a, b, (((1,), (1,)), ((), ())), preferred_element_type=F32)


def _ada_kernel(c_ref, w_ref, b_ref, o_ref):
    c = c_ref[...]
    s = (c * jax.nn.sigmoid(c)).astype(BF16)
    o_ref[...] = _dot(s, w_ref[...].astype(BF16)) + b_ref[...]


def _ada(c, w_ada, b_ada):
    m, d = c.shape
    n = w_ada.shape[1]
    tn = _tile(n, 512)
    return pl.pallas_call(
        _ada_kernel,
        grid=(n // tn,),
        in_specs=[pl.BlockSpec((m, d), lambda j: (0, 0)),
                  pl.BlockSpec((d, tn), lambda j: (0, j)),
                  pl.BlockSpec((1, tn), lambda j: (0, j))],
        out_specs=pl.BlockSpec((m, tn), lambda j: (0, j)),
        out_shape=jax.ShapeDtypeStruct((m, n), F32),
        compiler_params=_cparams(("arbitrary",)),
        name="ada",
    )(c, w_ada, b_ada.reshape(1, n))


def _ln_mod_kernel(x_ref, shift_ref, scale_ref, h_ref):
    h_ref[...] = (_ln(x_ref[...]) * (1.0 + scale_ref[...]) + shift_ref[...]).astype(h_ref.dtype)


def _mod_spec(d, comp, tiles_per_seq, row0=0):
    return pl.BlockSpec((None, 1, d), lambda i: ((row0 + i // tiles_per_seq) * 6 + comp, 0, 0))


def _ln_mod(x, mod3, row0, seq_len, shift_comp, scale_comp):
    t, d = x.shape
    tm = _tile(seq_len, 256)
    tps = seq_len // tm
    return pl.pallas_call(
        _ln_mod_kernel,
        grid=(t // tm,),
        in_specs=[pl.BlockSpec((tm, d), lambda i: (i, 0)),
                  _mod_spec(d, shift_comp, tps, row0),
                  _mod_spec(d, scale_comp, tps, row0)],
        out_specs=pl.BlockSpec((tm, d), lambda i: (i, 0)),
        out_shape=jax.ShapeDtypeStruct((t, d), BF16),
        compiler_params=_cparams(("parallel",)),
        name="ln_mod",
    )(x, mod3, mod3)


def _proj_kernel(*refs, rope, scale, n_out):
    x_ref, w_ref = refs[0], refs[1]
    outs = refs[len(refs) - n_out:]
    acc = _dot(x_ref[...], w_ref[...])
    if rope:
        c_ref, s1_ref, s2_ref = refs[2:5]
        cos, s1, s2 = c_ref[...], s1_ref[...], s2_ref[...]
        parts = []
        for g in range(acc.shape[1] // LANES):
            blk = acc[:, g * LANES:(g + 1) * LANES]
            half = ROPE_DIMS // 2
            parts.append(blk * cos + pltpu.roll(blk, LANES - half, 1) * s1 + pltpu.roll(blk, half, 1) * s2)
        acc = jnp.concatenate(parts, axis=1) if len(parts) > 1 else parts[0]
    if scale != 1.0:
        acc = acc * scale
    for o in outs:
        o[...] = acc.astype(o.dtype)


def _proj(x, w, out_dtypes, rope_tabs=None, scale=1.0, tm_pref=512, tn_pref=1024):
    m, k = x.shape
    n = w.shape[1]
    tm, tn = _tile(m, tm_pref), _tile(n, tn_pref)
    in_specs = [pl.BlockSpec((tm, k), lambda i, j: (i, 0)),
                pl.BlockSpec((k, tn), lambda i, j: (0, j))]
    args = [x, w]
    if rope_tabs is not None:
        in_specs += [pl.BlockSpec((tm, LANES), lambda i, j: (i, 0))] * 3
        args += list(rope_tabs)
    outs = pl.pallas_call(
        functools.partial(_proj_kernel, rope=rope_tabs is not None, scale=scale, n_out=len(out_dtypes)),
        grid=(m // tm, n // tn),
        in_specs=in_specs,
        out_specs=[pl.BlockSpec((tm, tn), lambda i, j: (i, j)) for _ in out_dtypes],
        out_shape=[jax.ShapeDtypeStruct((m, n), dt) for dt in out_dtypes],
        compiler_params=_cparams(("parallel", "arbitrary")),
        name="proj",
    )(*args)
    return outs


def _glu_kernel(x_ref, wg_ref, wu_ref, o_ref):
    x = x_ref[...]
    g = _dot(x, wg_ref[...])
    u = _dot(x, wu_ref[...])
    o_ref[...] = (g * jax.nn.sigmoid(g) * u).astype(o_ref.dtype)


def _glu(x, wg, wu):
    m, k = x.shape
    n = wg.shape[1]
    tm, tn = _tile(m, 512), _tile(n, 512)
    return pl.pallas_call(
        _glu_kernel,
        grid=(m // tm, n // tn),
        in_specs=[pl.BlockSpec((tm, k), lambda i, j: (i, 0)),
                  pl.BlockSpec((k, tn), lambda i, j: (0, j)),
                  pl.BlockSpec((k, tn), lambda i, j: (0, j))],
        out_specs=pl.BlockSpec((tm, tn), lambda i, j: (i, j)),
        out_shape=jax.ShapeDtypeStruct((m, n), BF16),
        compiler_params=_cparams(("parallel", "arbitrary")),
        name="shared_glu",
    )(x, wg, wu)


def _merge_kernel(od_ref, os_ref, wd_ref, ws_ref, gd_ref, gs_ref, o_ref):
    bd = _dot(od_ref[...], wd_ref[...])
    bs = _dot(os_ref[...], ws_ref[...])
    gd = jax.nn.sigmoid(gd_ref[...].astype(F32))
    gs = jax.nn.sigmoid(gs_ref[...].astype(F32))
    o_ref[...] = (gd * bd + gs * bs).astype(o_ref.dtype)


def _merge(od, osb, wbd, wbs, gd, gs):
    m, k = od.shape
    n = wbd.shape[1]
    tm, tn = _tile(m, 512), _tile(n, 1024)
    return pl.pallas_call(
        _merge_kernel,
        grid=(m // tm, n // tn),
        in_specs=[pl.BlockSpec((tm, k), lambda i, j: (i, 0)),
                  pl.BlockSpec((tm, k), lambda i, j: (i, 0)),
                  pl.BlockSpec((k, tn), lambda i, j: (0, j)),
                  pl.BlockSpec((k, tn), lambda i, j: (0, j)),
                  pl.BlockSpec((tm, tn), lambda i, j: (i, j)),
                  pl.BlockSpec((tm, tn), lambda i, j: (i, j))],
        out_specs=pl.BlockSpec((tm, tn), lambda i, j: (i, j)),
        out_shape=jax.ShapeDtypeStruct((m, n), BF16),
        compiler_params=_cparams(("parallel", "arbitrary")),
        name="merge",
    )(od, osb, wbd, wbs, gd, gs)


def _lambda(lam_ref, lam_init):
    a = jnp.sum(lam_ref[0:1, :] * lam_ref[1:2, :], axis=-1, keepdims=True)
    b = jnp.sum(lam_ref[2:3, :] * lam_ref[3:4, :], axis=-1, keepdims=True)
    return jnp.exp(a) - jnp.exp(b) + lam_init


def _split_components(q):
    lane = lax.broadcasted_iota(jnp.int32, q.shape, 1)
    zero = jnp.zeros_like(q)
    return jnp.where(lane < DIFF_DK, q, zero), jnp.where(lane >= DIFF_DK, q, zero)


def _diff_finish(o1, l1, o2, l2, lam, g, lam_init):
    o = o1 / l1 - lam * (o2 / l2)
    o = o * lax.rsqrt(jnp.mean(o * o, axis=-1, keepdims=True) + LN_EPS)
    return o * g * (1.0 - lam_init)


def _diff_prompt_kernel(lam_ref, g_ref, q_ref, k_ref, v_ref, o_ref, m_ref, l_ref, acc_ref, *, tq, tk, lam_init):
    i = pl.program_id(1)
    q1, q2 = _split_components(q_ref[...])
    m_ref[...] = jnp.full(m_ref.shape, -jnp.inf, F32)
    l_ref[...] = jnp.zeros(l_ref.shape, F32)
    acc_ref[...] = jnp.zeros(acc_ref.shape, F32)

    def block(start, keep):
        k = k_ref[pl.ds(start, tk), :]
        v = v_ref[pl.ds(start, tk), :]
        for comp, qc in enumerate((q1, q2)):
            s = _dot_t(qc, k)
            if keep is not None:
                s = jnp.where(keep[0], jnp.where(keep[1], s, -1e30), -1e30)
            m_old = m_ref[comp]
            m_new = jnp.maximum(m_old, jnp.max(s, axis=-1, keepdims=True))
            alpha = jnp.exp(m_old - m_new)
            p = jnp.exp(s - m_new)
            l_ref[comp] = alpha * l_ref[comp] + jnp.sum(p, axis=-1, keepdims=True)
            acc_ref[comp] = alpha * acc_ref[comp] + _dot(p.astype(BF16), v)
            m_ref[comp] = m_new

    n_full = (i * tq) // tk

    def body(j, carry):
        block(pl.multiple_of(j * tk, tk), None)
        return carry

    lax.fori_loop(0, n_full, body, 0)
    start = pl.multiple_of(jnp.maximum((i + 1) * tq - tk, 0), tq)
    lo = n_full * tk
    col = start + lax.broadcasted_iota(jnp.int32, (tq, tk), 1)
    row = i * tq + lax.broadcasted_iota(jnp.int32, (tq, 1), 0)
    hi = (row // CHUNK + 1) * CHUNK
    block(start, (col >= lo, col < hi))
    lam = _lambda(lam_ref, lam_init)
    o = _diff_finish(acc_ref[0], l_ref[0], acc_ref[1], l_ref[1], lam, g_ref[...], lam_init)
    o_ref[...] = o.astype(o_ref.dtype)


def _diff_prompt(q, k, v, lam4, g, lam_init):
    t, w = q.shape
    heads = w // HEAD_W
    tq = _tile(t, 256)
    tk = _tile(t, 1024)
    assert tq % CHUNK == 0 and tk % tq == 0
    return pl.pallas_call(
        functools.partial(_diff_prompt_kernel, tq=tq, tk=tk, lam_init=lam_init),
        grid=(heads, t // tq),
        in_specs=[pl.BlockSpec((4, DIFF_DK), lambda h, i: (0, 0)),
                  pl.BlockSpec((1, HEAD_W), lambda h, i: (0, 0)),
                  pl.BlockSpec((tq, HEAD_W), lambda h, i: (i, h)),
                  pl.BlockSpec((t, HEAD_W), lambda h, i: (0, h)),
                  pl.BlockSpec((t, HEAD_W), lambda h, i: (0, h))],
        out_specs=pl.BlockSpec((tq, HEAD_W), lambda h, i: (i, h)),
        out_shape=jax.ShapeDtypeStruct((t, w), BF16),
        scratch_shapes=[pltpu.VMEM((2, tq, 1), F32), pltpu.VMEM((2, tq, 1), F32),
                        pltpu.VMEM((2, tq, HEAD_W), F32)],
        compiler_params=_cparams(("parallel", "arbitrary")),
        name="diff_prompt",
    )(lam4, g, q, k, v)


def _diff_sample_kernel(lam_ref, g_ref, q_ref, kn_ref, vn_ref, kct_ref, vc_ref, o_ref, *, past_len, lam_init):
    q1, q2 = _split_components(q_ref[...])
    kpt = kct_ref[...].astype(BF16)
    vp = vc_ref[...].astype(BF16)
    kn = kn_ref[...].astype(BF16)
    vn = vn_ref[...].astype(BF16)
    tn = kn.shape[0]
    r = (past_len + lax.broadcasted_iota(jnp.int32, (tn, tn), 0)) // CHUNK
    c = (past_len + lax.broadcasted_iota(jnp.int32, (tn, tn), 1)) // CHUNK
    keep = c <= r
    res = []
    for qc in (q1, q2):
        sp = _dot(qc, kpt)
        sn = jnp.where(keep, _dot_t(qc, kn), -1e30)
        m = jnp.maximum(jnp.max(sp, axis=-1, keepdims=True), jnp.max(sn, axis=-1, keepdims=True))
        pp = jnp.exp(sp - m)
        pn = jnp.exp(sn - m)
        l = jnp.sum(pp, axis=-1, keepdims=True) + jnp.sum(pn, axis=-1, keepdims=True)
        res.append((_dot(pp.astype(BF16), vp) + _dot(pn.astype(BF16), vn), l))
    lam = _lambda(lam_ref, lam_init)
    o = _diff_finish(res[0][0], res[0][1], res[1][0], res[1][1], lam, g_ref[...], lam_init)
    o_ref[...] = o.astype(o_ref.dtype)


def _diff_sample(q, kn, vn, kct, vc, lam4, g, lam_init, seq_len):
    t, w = q.shape
    b, heads, _, past_len = kct.shape
    assert past_len % CHUNK == 0
    row = pl.BlockSpec((seq_len, HEAD_W), lambda bi, h: (bi, h))
    cache = pl.BlockSpec((None, past_len, HEAD_W), lambda bi, h: (bi, 0, h))
    cache_t = pl.BlockSpec((None, None, HEAD_W, past_len), lambda bi, h: (bi, h, 0, 0))
    return pl.pallas_call(
        functools.partial(_diff_sample_kernel, past_len=past_len, lam_init=lam_init),
        grid=(b, heads),
        in_specs=[pl.BlockSpec((4, DIFF_DK), lambda bi, h: (0, 0)),
                  pl.BlockSpec((1, HEAD_W), lambda bi, h: (0, 0)),
                  row, row, row, cache_t, cache],
        out_specs=row,
        out_shape=jax.ShapeDtypeStruct((t, w), BF16),
        compiler_params=_cparams(("parallel", "arbitrary")),
        name="diff_sample",
    )(lam4, g, q, kn, vn, kct, vc)


def _head_rows(cache_ref, head, start, size, heads):
    return cache_ref[pl.ds(start * heads + head, size, stride=heads), :]


def _head_major(x, b, s):
    return x.reshape(b, s, -1, HEAD_W).transpose(0, 2, 1, 3)


def _token_major(x):
    b, h, s, w = x.shape
    return x.transpose(0, 2, 1, 3).reshape(b * s, h * w)


def _sample_specs(seq_len, heads, pc, chunk_index):
    row = pl.BlockSpec((None, heads, seq_len, HEAD_W), lambda bi, c: (bi, 0, 0, 0))
    cache = pl.BlockSpec((None, pc * heads, HEAD_W), lambda bi, c: (bi, chunk_index(c), 0))
    return row, cache


def _sb_block(q, k, v, carry_ref, acc_ref, keep):
    tk = k.shape[0]
    z = _dot_t(q, k)
    soft = jnp.log(1.0 + jnp.exp(-jnp.abs(z)))
    log_sig = jnp.minimum(z, 0.0) - soft
    log_keep = jnp.minimum(-z, 0.0) - soft
    if keep is not None:
        log_keep = jnp.where(keep, log_keep, 0.0)
    upper = (lax.broadcasted_iota(jnp.int32, (tk, tk), 0) > lax.broadcasted_iota(jnp.int32, (tk, tk), 1))
    upper = jnp.where(upper, 1.0, 0.0).astype(BF16)
    hi = log_keep.astype(BF16)
    lo = (log_keep - hi.astype(F32)).astype(BF16)
    later = _dot(hi, upper) + _dot(lo, upper) + carry_ref[...]
    w = jnp.exp(log_sig + later)
    if keep is not None:
        w = jnp.where(keep, w, 0.0)
    acc_ref[...] += _dot(w.astype(BF16), v)
    carry_ref[...] += jnp.sum(log_keep, axis=-1, keepdims=True)


def _sb_prompt_kernel(q_ref, k_ref, v_ref, o_ref, carry_ref, acc_ref, *, tq):
    i = pl.program_id(1)
    q = q_ref[...]
    carry_ref[...] = jnp.zeros(carry_ref.shape, F32)
    acc_ref[...] = jnp.zeros(acc_ref.shape, F32)

    def load(j):
        start = pl.multiple_of(j * tq, tq)
        return k_ref[pl.ds(start, tq), :], v_ref[pl.ds(start, tq), :]

    strict = (lax.broadcasted_iota(jnp.int32, (tq, tq), 1) < lax.broadcasted_iota(jnp.int32, (tq, tq), 0))
    k, v = load(i)
    _sb_block(q, k, v, carry_ref, acc_ref, strict)

    def cond(state):
        j, alive = state
        return jnp.logical_and(j >= 0, alive > SB_DEAD_LOG)

    def body(state):
        j, _ = state
        kj, vj = load(j)
        _sb_block(q, kj, vj, carry_ref, acc_ref, None)
        return j - 1, jnp.max(carry_ref[...])

    lax.while_loop(cond, body, (i - 1, jnp.max(carry_ref[...])))
    o_ref[...] = acc_ref[...].astype(o_ref.dtype)


def _sb_prompt(q, k, v):
    t, w = q.shape
    heads = w // HEAD_W
    tq = _tile(t, 256)
    return pl.pallas_call(
        functools.partial(_sb_prompt_kernel, tq=tq),
        grid=(heads, t // tq),
        in_specs=[pl.BlockSpec((tq, HEAD_W), lambda h, i: (i, h)),
                  pl.BlockSpec((t, HEAD_W), lambda h, i: (0, h)),
                  pl.BlockSpec((t, HEAD_W), lambda h, i: (0, h))],
        out_specs=pl.BlockSpec((tq, HEAD_W), lambda h, i: (i, h)),
        out_shape=jax.ShapeDtypeStruct((t, w), BF16),
        scratch_shapes=[pltpu.VMEM((tq, 1), F32), pltpu.VMEM((tq, HEAD_W), F32)],
        compiler_params=_cparams(("parallel", "arbitrary")),
        name="sb_prompt",
    )(q, k, v)


def _sb_sample_kernel(q_ref, kn_ref, vn_ref, kc_ref, vc_ref, o_ref, carry_ref, acc_ref, *, heads, pc, tk):
    c = pl.program_id(1)
    tn = q_ref.shape[1]

    @pl.when(c == 0)
    def _():
        carry_ref[...] = jnp.zeros(carry_ref.shape, F32)
        acc_ref[...] = jnp.zeros(acc_ref.shape, F32)
        strict = (lax.broadcasted_iota(jnp.int32, (tn, tn), 1) < lax.broadcasted_iota(jnp.int32, (tn, tn), 0))

        def new_rows(h, carry):
            _sb_block(q_ref[h], kn_ref[h].astype(BF16), vn_ref[h].astype(BF16), carry_ref.at[h], acc_ref.at[h], strict)
            return carry

        lax.fori_loop(0, heads, new_rows, 0)

    for j in reversed(range(pc // tk)):
        @pl.when(jnp.max(carry_ref[...]) > SB_DEAD_LOG)
        def _():
            def past_rows(h, carry):
                kj = _head_rows(kc_ref, h, j * tk, tk, heads).astype(BF16)
                vj = _head_rows(vc_ref, h, j * tk, tk, heads).astype(BF16)
                _sb_block(q_ref[h], kj, vj, carry_ref.at[h], acc_ref.at[h], None)
                return carry

            lax.fori_loop(0, heads, past_rows, 0)

    @pl.when(c == pl.num_programs(1) - 1)
    def _():
        o_ref[...] = acc_ref[...].astype(o_ref.dtype)


def _sb_sample(q, kn, vn, kc, vc, seq_len):
    t, w = q.shape
    heads = w // HEAD_W
    b = kc.shape[0]
    past_len = kc.shape[1] // heads
    pc = _tile(past_len, SAMPLE_PAST_CHUNK)
    n_c = past_len // pc
    tk = _tile(pc, 256)
    row, cache = _sample_specs(seq_len, heads, pc, lambda c: n_c - 1 - c)
    out = pl.pallas_call(
        functools.partial(_sb_sample_kernel, heads=heads, pc=pc, tk=tk),
        grid=(b, n_c),
        in_specs=[row, row, row, cache, cache],
        out_specs=row,
        out_shape=jax.ShapeDtypeStruct((b, heads, seq_len, HEAD_W), BF16),
        scratch_shapes=[pltpu.VMEM((heads, seq_len, 1), F32), pltpu.VMEM((heads, seq_len, HEAD_W), F32)],
        compiler_params=_cparams(("parallel", "arbitrary")),
        name="sb_sample",
    )(_head_major(q, b, seq_len), _head_major(kn, b, seq_len), _head_major(vn, b, seq_len), kc, vc)
    return _token_major(out)


def _pack_halves(x):
    n = x.shape[1] // 2
    bits = pltpu.bitcast(x.astype(BF16).astype(F32), jnp.uint32)
    return bits[:, n:] | (bits[:, :n] >> 16)


def _unpack_halves(u):
    lo = pltpu.bitcast(u << 16, F32)
    hi = pltpu.bitcast(u & jnp.uint32(0xFFFF0000), F32)
    return lo, hi


def _resid_norm_kernel(x_ref, a_ref, gate_ref, g_ref, b_ref, shift_ref, scale_ref, x1_ref, h_ref, hf_ref, *, alpha):
    x1 = _ln(alpha * x_ref[...] + gate_ref[...] * a_ref[...]) * g_ref[...] + b_ref[...]
    x1_ref[...] = x1
    h = _ln(x1) * (1.0 + scale_ref[...]) + shift_ref[...]
    h_ref[...] = h.astype(h_ref.dtype)
    hf_ref[...] = h


def _resid_norm(x, a, mod3, row0, seq_len, g, b, alpha):
    t, d = x.shape
    tm = _tile(seq_len, 256)
    tps = seq_len // tm
    tok = pl.BlockSpec((tm, d), lambda i: (i, 0))
    vec = pl.BlockSpec((1, d), lambda i: (0, 0))
    return pl.pallas_call(
        functools.partial(_resid_norm_kernel, alpha=alpha),
        grid=(t // tm,),
        in_specs=[tok, tok, _mod_spec(d, 2, tps, row0), vec, vec,
                  _mod_spec(d, 3, tps, row0), _mod_spec(d, 4, tps, row0)],
        out_specs=[tok, tok, tok],
        out_shape=[jax.ShapeDtypeStruct((t, d), F32), jax.ShapeDtypeStruct((t, d), BF16),
                   jax.ShapeDtypeStruct((t, d), F32)],
        compiler_params=_cparams(("parallel",)),
        name="resid_norm",
    )(x, a, mod3, g, b, mod3, mod3)


def _router_kernel(h_ref, whi_ref, wlo_ref, bias_ref, gate_ref, idx_ref, rank_ref, cnt_ref, run_ref):
    @pl.when(pl.program_id(0) == 0)
    def _():
        run_ref[...] = jnp.zeros(run_ref.shape, F32)

    h = h_ref[...]
    logits = _dot(h, whi_ref[...]) + _dot(h, wlo_ref[...])
    scores = jax.nn.sigmoid(logits)
    tm, e = scores.shape
    lane = lax.broadcasted_iota(jnp.int32, (tm, e), 1)
    sel = scores + bias_ref[...]
    picked = jnp.zeros((tm, e), F32)
    idx = jnp.zeros((tm, e), jnp.int32)
    firsts = []
    for k in range(TOP_K):
        best = jnp.max(sel, axis=-1, keepdims=True)
        first = jnp.min(jnp.where(sel == best, lane, e), axis=-1, keepdims=True)
        hit = lane == first
        picked = jnp.where(hit, 1.0, picked)
        sel = jnp.where(hit, -jnp.inf, sel)
        idx = jnp.where(lane == k, first, idx)
        firsts.append(first)
    gate = scores * picked
    gate = gate / jnp.sum(gate, axis=-1, keepdims=True) * ROUTE_SCALE
    lower = (lax.broadcasted_iota(jnp.int32, (tm, tm), 1) < lax.broadcasted_iota(jnp.int32, (tm, tm), 0))
    within = _dot(jnp.where(lower, 1.0, 0.0).astype(BF16), picked.astype(BF16)) + run_ref[...]
    gate8 = jnp.zeros((tm, e), F32)
    rank8 = jnp.zeros((tm, e), F32)
    for k, first in enumerate(firsts):
        hit = lane == first
        gk = jnp.sum(jnp.where(hit, gate, 0.0), axis=-1, keepdims=True)
        rk = jnp.sum(jnp.where(hit, within, 0.0), axis=-1, keepdims=True)
        gate8 = jnp.where(lane == k, gk, gate8)
        rank8 = jnp.where(lane == k, rk, rank8)
    gate_ref[...] = gate8
    idx_ref[...] = idx
    rank_ref[...] = rank8.astype(jnp.int32)
    run_ref[...] += jnp.sum(picked, axis=0, keepdims=True)
    cnt_ref[...] = run_ref[...].astype(jnp.int32)


def _router(h, w_router, router_bias):
    t, d = h.shape
    e = w_router.shape[1]
    assert e == LANES
    tm = _tile(t, 512)
    whi = w_router.astype(BF16)
    wlo = (w_router - whi.astype(F32)).astype(BF16)
    tok = pl.BlockSpec((tm, e), lambda i: (i, 0))
    wspec = pl.BlockSpec((d, e), lambda i: (0, 0))
    one = pl.BlockSpec((1, e), lambda i: (0, 0))
    return pl.pallas_call(
        _router_kernel,
        grid=(t // tm,),
        in_specs=[pl.BlockSpec((tm, d), lambda i: (i, 0)), wspec, wspec, one],
        out_specs=[tok, tok, tok, one],
        out_shape=[jax.ShapeDtypeStruct((t, e), F32), jax.ShapeDtypeStruct((t, e), jnp.int32),
                   jax.ShapeDtypeStruct((t, e), jnp.int32), jax.ShapeDtypeStruct((1, e), jnp.int32)],
        scratch_shapes=[pltpu.VMEM((1, e), F32)],
        compiler_params=_cparams(("arbitrary",)),
        name="router",
    )(h, whi, wlo, router_bias.reshape(1, e))


def _row_copy(src_hbm, tok, buf, slot, r, sem):
    return pltpu.make_async_copy(src_hbm.at[pl.ds(tok, 1)], buf.at[slot, pl.ds(r, 1)], sem.at[slot])


def _expert_weights(plan_ref, b, w_hbms, w_bufs, w_sem):
    expert, first, slot = plan_ref[0, b], plan_ref[1, b], plan_ref[2, b]
    has_next, nxt = plan_ref[3, b], plan_ref[4, b]

    def copies(e, into):
        return [pltpu.make_async_copy(w.at[e], buf.at[into], w_sem.at[into, n])
                for n, (w, buf) in enumerate(zip(w_hbms, w_bufs))]

    @pl.when(b == 0)
    def _():
        for cp in copies(expert, slot):
            cp.start(priority=1)

    @pl.when(first == 1)
    def _():
        for cp in copies(expert, slot):
            cp.wait()

    @pl.when(jnp.logical_and(first == 1, has_next == 1))
    def _():
        for cp in copies(nxt, 1 - slot):
            cp.start(priority=1)

    return slot


def _expert_up_kernel(plan_ref, nused_ref, tok_ref, tok_next_ref, h_hbm, wg_hbm, wu_hbm, o_ref, buf, sem,
                      wg_buf, wu_buf, w_sem, *, blk, n_blocks):
    b = pl.program_id(0)
    n_used = nused_ref[0]
    slot = lax.rem(b, 2)

    def start_rows(toks, into):
        for r in range(blk):
            _row_copy(h_hbm, toks[0, r], buf, into, r, sem).start()

    def wait_rows(into):
        for r in range(blk):
            _row_copy(h_hbm, 0, buf, into, r, sem).wait()

    @pl.when(b == 0)
    def _():
        def body(r, carry):
            _row_copy(h_hbm, tok_ref[0, r], buf, slot, r, sem).start()
            return carry
        lax.fori_loop(0, blk, body, 0)

    @pl.when(b < n_used)
    def _():
        ws = _expert_weights(plan_ref, b, (wg_hbm, wu_hbm), (wg_buf, wu_buf), w_sem)
        wait_rows(slot)
        kc = buf.shape[2] // GATHER_BATCHES
        rows = blk // GATHER_BATCHES
        g = u = None
        for c in range(GATHER_BATCHES):
            for r in range(c * rows, (c + 1) * rows):
                _row_copy(h_hbm, tok_next_ref[0, r], buf, 1 - slot, r, sem).start()
            xc = buf[slot, :, c * kc:(c + 1) * kc].astype(BF16)
            gc = _dot(xc, wg_buf[ws, c * kc:(c + 1) * kc, :].astype(BF16))
            uc = _dot(xc, wu_buf[ws, c * kc:(c + 1) * kc, :].astype(BF16))
            g = gc if g is None else g + gc
            u = uc if u is None else u + uc
        o_ref[...] = (g * jax.nn.sigmoid(g) * u).astype(o_ref.dtype)

        @pl.when(b == n_blocks - 1)
        def _():
            wait_rows(1 - slot)

    @pl.when(b == n_used)
    def _():
        wait_rows(slot)

    @pl.when(b >= n_used)
    def _():
        o_ref[...] = jnp.zeros(o_ref.shape, o_ref.dtype)


def _expert_up(h, wg, wu, slot_tok, plan, n_used, blk):
    e, d, hid = wg.shape
    n_blocks = plan.shape[1]
    tok3 = slot_tok.reshape(n_blocks, 1, blk)
    grid_spec = pltpu.PrefetchScalarGridSpec(
        num_scalar_prefetch=2,
        grid=(n_blocks,),
        in_specs=[pl.BlockSpec((None, 1, blk), lambda b, pr, nu: (b, 0, 0), memory_space=pltpu.SMEM),
                  pl.BlockSpec((None, 1, blk), lambda b, pr, nu: (jnp.minimum(b + 1, n_blocks - 1), 0, 0),
                               memory_space=pltpu.SMEM),
                  pl.BlockSpec(memory_space=pl.ANY), pl.BlockSpec(memory_space=pl.ANY),
                  pl.BlockSpec(memory_space=pl.ANY)],
        out_specs=pl.BlockSpec((blk, hid), lambda b, pr, nu: (b, 0)),
        scratch_shapes=[pltpu.VMEM((2, blk, d), F32), pltpu.SemaphoreType.DMA((2,)),
                        pltpu.VMEM((2, d, hid), F32), pltpu.VMEM((2, d, hid), F32),
                        pltpu.SemaphoreType.DMA((2, 2))],
    )
    return pl.pallas_call(
        functools.partial(_expert_up_kernel, blk=blk, n_blocks=n_blocks),
        grid_spec=grid_spec,
        out_shape=jax.ShapeDtypeStruct((n_blocks * blk, hid), BF16),
        compiler_params=_cparams(("arbitrary",)),
        name="expert_up",
    )(plan, n_used, tok3, tok3, h, wg, wu)


def _expert_down_kernel(plan_ref, nused_ref, a_ref, wd_hbm, o_ref, wd_buf, w_sem):
    b = pl.program_id(0)

    @pl.when(b < nused_ref[0])
    def _():
        ws = _expert_weights(plan_ref, b, (wd_hbm,), (wd_buf,), w_sem)
        o_ref[...] = _pack_halves(_dot(a_ref[...], wd_buf[ws].astype(BF16)))

    @pl.when(b >= nused_ref[0])
    def _():
        o_ref[...] = jnp.zeros(o_ref.shape, o_ref.dtype)


def _expert_down(act, wd, plan, n_used, blk):
    e, hid, d = wd.shape
    n_blocks = plan.shape[1]
    grid_spec = pltpu.PrefetchScalarGridSpec(
        num_scalar_prefetch=2,
        grid=(n_blocks,),
        in_specs=[pl.BlockSpec((blk, hid), lambda b, pr, nu: (b, 0)),
                  pl.BlockSpec(memory_space=pl.ANY)],
        out_specs=pl.BlockSpec((blk, d // 2), lambda b, pr, nu: (b, 0)),
        scratch_shapes=[pltpu.VMEM((2, hid, d), F32), pltpu.SemaphoreType.DMA((2, 1))],
    )
    return pl.pallas_call(
        _expert_down_kernel,
        grid_spec=grid_spec,
        out_shape=jax.ShapeDtypeStruct((n_blocks * blk, d // 2), jnp.uint32),
        compiler_params=_cparams(("arbitrary",)),
        name="expert_down",
    )(plan, n_used, act, wd)


def _combine_kernel(dest_ref, dest_next_ref, y_hbm, gate_ref, sh_ref, x_ref, gf_ref, g_ref, b_ref, o_ref, buf, sem, *,
                    tt, n_steps, alpha):
    i = pl.program_id(0)
    slot = lax.rem(i, 2)

    def copy(src_row, into, r, k):
        return pltpu.make_async_copy(y_hbm.at[pl.ds(src_row, 1)], buf.at[into, k, pl.ds(r, 1)], sem.at[into])

    def start_rows(dref, into):
        for r in range(tt):
            for k in range(TOP_K):
                copy(dref[0, r * TOP_K + k], into, r, k).start()

    def wait_rows(into):
        for r in range(tt):
            for k in range(TOP_K):
                copy(0, into, r, k).wait()

    @pl.when(i == 0)
    def _():
        def body(r, carry):
            for k in range(TOP_K):
                copy(dest_ref[0, r * TOP_K + k], slot, r, k).start()
            return carry
        lax.fori_loop(0, tt, body, 0)

    wait_rows(slot)
    start_rows(dest_next_ref, 1 - slot)
    gate = gate_ref[...]
    n = buf.shape[-1]
    f_lo, f_hi = sh_ref[:, :n], sh_ref[:, n:]
    for k in range(TOP_K):
        y_lo, y_hi = _unpack_halves(buf[slot, k])
        f_lo = f_lo + y_lo * gate[:, k:k + 1]
        f_hi = f_hi + y_hi * gate[:, k:k + 1]
    f = jnp.concatenate([f_lo, f_hi], axis=1)
    o_ref[...] = _ln(alpha * x_ref[...] + gf_ref[...] * f) * g_ref[...] + b_ref[...]

    @pl.when(i == n_steps - 1)
    def _():
        wait_rows(1 - slot)


def _combine(y_slots, dest8, gate8, shared, x1, mod3, row0, seq_len, tok0, n_tok, g, b, alpha):
    d = x1.shape[1]
    tt = _tile(seq_len, 64)
    assert tok0 % tt == 0
    tile0 = tok0 // tt
    tps = seq_len // tt
    n_steps = n_tok // tt
    tok = pl.BlockSpec((tt, d), lambda i: (tile0 + i, 0))
    vec = pl.BlockSpec((1, d), lambda i: (0, 0))
    dest3 = dest8.reshape(-1, 1, tt * TOP_K)
    return pl.pallas_call(
        functools.partial(_combine_kernel, tt=tt, n_steps=n_steps, alpha=alpha),
        grid=(n_steps,),
        in_specs=[pl.BlockSpec((None, 1, tt * TOP_K), lambda i: (tile0 + i, 0, 0), memory_space=pltpu.SMEM),
                  pl.BlockSpec((None, 1, tt * TOP_K), lambda i: (tile0 + jnp.minimum(i + 1, n_steps - 1), 0, 0),
                               memory_space=pltpu.SMEM),
                  pl.BlockSpec(memory_space=pl.ANY),
                  pl.BlockSpec((tt, LANES), lambda i: (tile0 + i, 0)),
                  tok, tok,
                  pl.BlockSpec((None, 1, d), lambda i: ((row0 + i // tps) * 6 + 5, 0, 0)),
                  vec, vec],
        out_specs=pl.BlockSpec((tt, d), lambda i: (i, 0)),
        out_shape=jax.ShapeDtypeStruct((n_tok, d), F32),
        scratch_shapes=[pltpu.VMEM((2, TOP_K, tt, d // 2), jnp.uint32), pltpu.SemaphoreType.DMA((2,))],
        compiler_params=_cparams(("arbitrary",)),
        name="combine",
    )(dest3, dest3, y_slots, gate8, shared, x1, mod3, g, b)


def _rope_tables(pos):
    half = ROPE_DIMS // 2
    inv_freq = ROPE_THETA ** (-jnp.arange(half, dtype=F32) / half)
    ang = pos.astype(F32)[:, None] * inv_freq[None, :]
    cos, sin = jnp.cos(ang), jnp.sin(ang)
    n = pos.shape[0]
    ones = jnp.ones((n, DIFF_DK - ROPE_DIMS), F32)
    zeros = jnp.zeros((n, DIFF_DK - ROPE_DIMS), F32)
    zh = jnp.zeros((n, half), F32)
    c64 = jnp.concatenate([cos, cos, ones], axis=1)
    s1 = jnp.concatenate([-sin, zh, zeros], axis=1)
    s2 = jnp.concatenate([zh, sin, zeros], axis=1)
    rep = LANES // DIFF_DK
    return tuple(jnp.tile(a, (1, rep)) for a in (c64, s1, s2))


def _mixer(x, mod3, row0, seq_len, past, wts, lam4, g_norm, lam_init):
    (w_qd, w_kd, w_vd, w_qs, w_ks, w_vs, w_gd, w_gs, w_bd, w_bs, w_out) = wts
    t = x.shape[0]
    past_len = 0 if past is None else past[0].shape[-1]
    pos = past_len + jnp.arange(seq_len, dtype=jnp.int32)
    tabs = _rope_tables(jnp.tile(pos, t // seq_len))
    h = _ln_mod(x, mod3, row0, seq_len, 0, 1)
    (qd,) = _proj(h, w_qd, (BF16,), rope_tabs=tabs, scale=DIFF_DK ** -0.5)
    (qs,) = _proj(h, w_qs, (BF16,), scale=HEAD_W ** -0.5)
    (gd,) = _proj(h, w_gd, (BF16,))
    (gs,) = _proj(h, w_gs, (BF16,))
    if past is None:
        kd, kd_b = _proj(h, w_kd, (F32, BF16), rope_tabs=tabs)
        vd, vd_b = _proj(h, w_vd, (F32, BF16))
        ks, ks_b = _proj(h, w_ks, (F32, BF16))
        vs, vs_b = _proj(h, w_vs, (F32, BF16))
        od = _diff_prompt(qd, kd_b, vd_b, lam4, g_norm, lam_init)
        osb = _sb_prompt(qs, ks_b, vs_b)
    else:
        (kd,) = _proj(h, w_kd, (F32,), rope_tabs=tabs)
        (vd,) = _proj(h, w_vd, (F32,))
        (ks,) = _proj(h, w_ks, (F32,))
        (vs,) = _proj(h, w_vs, (F32,))
        od = _diff_sample(qd, kd, vd, past[0], past[1], lam4, g_norm, lam_init, seq_len)
        osb = _sb_sample(qs, ks, vs, past[2], past[3], seq_len)
    merged = _merge(od, osb, w_bd, w_bs, gd, gs)
    (a,) = _proj(merged, w_out, (F32,))
    return a, (kd, vd, ks, vs)


def _moe_plan(idx8, rank8, counts, n_tok, blk):
    e = counts.shape[0]
    padded = (counts + blk - 1) // blk * blk
    pad_ends = jnp.cumsum(padded)
    pad_starts = pad_ends - padded
    hit = idx8[..., None] == jnp.arange(e, dtype=jnp.int32)
    dest8 = (jnp.sum(jnp.where(hit, pad_starts, 0), axis=-1) + rank8).astype(jnp.int32)
    n_blocks = -(-n_tok * TOP_K // blk) + e
    tok_ids = jnp.repeat(jnp.arange(n_tok, dtype=jnp.int32), TOP_K)
    slot_tok = jnp.zeros((n_blocks * blk,), jnp.int32).at[dest8.reshape(-1)].set(tok_ids)
    block_pos = jnp.arange(n_blocks, dtype=jnp.int32) * blk
    block_expert = jnp.minimum(jnp.searchsorted(pad_ends, block_pos, side='right'), e - 1).astype(jnp.int32)
    n_used = (pad_ends[-1] // blk).astype(jnp.int32).reshape(1)
    bidx = jnp.arange(n_blocks, dtype=jnp.int32)
    first = jnp.concatenate([jnp.ones((1,), bool), block_expert[1:] != block_expert[:-1]]) & (bidx < n_used[0])
    wslot = (jnp.cumsum(first.astype(jnp.int32)) - 1) % 2
    first_pos = jnp.where(first, bidx, n_blocks)
    later_first = jnp.concatenate([lax.cummin(first_pos, reverse=True)[1:], jnp.full((1,), n_blocks, jnp.int32)])
    has_next = later_first < n_blocks
    next_expert = block_expert[jnp.minimum(later_first, n_blocks - 1)]
    plan = jnp.stack([block_expert, first.astype(jnp.int32), wslot.astype(jnp.int32), has_next.astype(jnp.int32),
                      next_expert]).astype(jnp.int32)
    return dest8, slot_tok, plan, n_used


def kernel(x_prompt, x_sample, c_prompt, c_sample, cache_diff_k, cache_diff_v, cache_sb_k, cache_sb_v, w_ada, b_ada, w_in, w_br_diff, w_br_sb, w_out, lam_q1, lam_k1, lam_q2, lam_k2, diff_norm_g, ln1_g, ln1_b, w_router, router_bias, w_e_gate, w_e_up, w_e_down, w_sh_gate, w_sh_up, w_sh_down, ln2_g, ln2_b):
    depth = w_ada.shape[0]
    assert depth == 1
    bp, sp, d = x_prompt.shape
    bs, ss, _ = x_sample.shape
    assert bp == 1
    diff_w = w_br_diff.shape[1]
    sb_w = w_br_sb.shape[1]
    heads_d, heads_s = diff_w // HEAD_W, sb_w // HEAD_W
    past_len = cache_diff_k.shape[2]
    alpha = (2.0 * depth) ** 0.25
    lam_init = 0.8 - 0.6 * float(np.exp(-0.3 * 0))
    blk = 256

    widths = [diff_w, diff_w, diff_w, sb_w, sb_w, sb_w, d, d]
    offs = np.concatenate([[0], np.cumsum(widths)])
    w_in0 = w_in[0]
    w_qd, w_kd, w_vd, w_qs, w_ks, w_vs, w_gd, w_gs = (
        w_in0[:, offs[i]:offs[i + 1]].astype(BF16) for i in range(8))
    wts = (w_qd, w_kd, w_vd, w_qs, w_ks, w_vs, w_gd, w_gs,
           w_br_diff[0].astype(BF16), w_br_sb[0].astype(BF16), w_out[0].astype(BF16))
    lam4 = jnp.stack([lam_q1[0], lam_k1[0], lam_q2[0], lam_k2[0]]).astype(F32)
    g_norm = diff_norm_g[0].reshape(1, HEAD_W).astype(F32)

    c_all = jnp.concatenate([c_prompt, c_sample], axis=0)
    n_seq = c_all.shape[0]
    c_pad = jnp.pad(c_all, ((0, -n_seq % 8), (0, 0)))
    mod = _ada(c_pad, w_ada[0], b_ada[0])
    mod3 = mod.reshape(c_pad.shape[0] * 6, 1, d)

    xp = x_prompt.reshape(bp * sp, d)
    xs = x_sample.reshape(bs * ss, d)
    past = (jnp.transpose(cache_diff_k[0], (0, 2, 3, 4, 1)).reshape(bs, heads_d, HEAD_W, past_len),
            cache_diff_v[0].reshape(bs, past_len, diff_w),
            cache_sb_k[0].reshape(bs, past_len * heads_s, HEAD_W), cache_sb_v[0].reshape(bs, past_len * heads_s, HEAD_W))

    a_p, rows_p = _mixer(xp, mod3, 0, sp, None, wts, lam4, g_norm, lam_init)
    a_s, rows_s = _mixer(xs, mod3, bp, ss, past, wts, lam4, g_norm, lam_init)
    g1, b1 = ln1_g[0].reshape(1, d), ln1_b[0].reshape(1, d)
    x1_p, h2_p, h2f_p = _resid_norm(xp, a_p, mod3, 0, sp, g1, b1, alpha)
    x1_s, h2_s, h2f_s = _resid_norm(xs, a_s, mod3, bp, ss, g1, b1, alpha)

    n_p, n_s = bp * sp, bs * ss
    n_tok = n_p + n_s
    h2 = jnp.concatenate([h2_p, h2_s], axis=0)
    h2f = jnp.concatenate([h2f_p, h2f_s], axis=0)
    x1 = jnp.concatenate([x1_p, x1_s], axis=0)
    gate8, idx8, rank8, counts = _router(h2, w_router[0], router_bias[0])
    dest8, slot_tok, plan, n_used = _moe_plan(idx8[:, :TOP_K], rank8[:, :TOP_K], counts[0], n_tok, blk)
    act = _expert_up(h2f, w_e_gate[0], w_e_up[0], slot_tok, plan, n_used, blk)
    y_slots = _expert_down(act, w_e_down[0], plan, n_used, blk)
    sh_act = _glu(h2, w_sh_gate[0].astype(BF16), w_sh_up[0].astype(BF16))
    (shared,) = _proj(sh_act, w_sh_down[0].astype(BF16), (F32,))
    g2, b2 = ln2_g[0].reshape(1, d), ln2_b[0].reshape(1, d)
    y_p = _combine(y_slots, dest8, gate8, shared, x1, mod3, 0, sp, 0, n_p, g2, b2, alpha)
    y_s = _combine(y_slots, dest8, gate8, shared, x1, mod3, bp, ss, n_p, n_s, g2, b2, alpha)

    def rows(r, b, s):
        kd, vd, ks, vs = r
        return (kd.reshape(1, b, s, heads_d, 2, DIFF_DK), vd.reshape(1, b, s, heads_d, HEAD_W),
                ks.reshape(1, b, s, heads_s, HEAD_W), vs.reshape(1, b, s, heads_s, HEAD_W))

    return (y_p.reshape(bp, sp, d), y_s.reshape(bs, ss, d)) + rows(rows_p, bp, sp) + rows(rows_s, bs, ss)
```

```python
import functools

import jax
import jax.numpy as jnp
import numpy as np
from jax import lax
from jax.experimental import pallas as pl
from jax.experimental.pallas import tpu as pltpu

F32 = jnp.float32
BF16 = jnp.bfloat16

CHUNK = 64
DIFF_DK = 64
HEAD_W = 128
ROPE_THETA = 500000.0
ROPE_DIMS = DIFF_DK // 4
TOP_K = 8
ROUTE_SCALE = 2.5
LN_EPS = 1e-5
LANES = 128
VMEM_LIMIT = 52 * 1024 * 1024
SB_DEAD_LOG = -104.0
GATHER_BATCHES = 4
SAMPLE_PAST_CHUNK = 1024


def _cparams(sem):
    return pltpu.CompilerParams(dimension_semantics=sem, vmem_limit_bytes=VMEM_LIMIT)


def _tile(n, pref):
    if n <= pref:
        return n
    t = pref - pref % 64
    while n % t:
        t -= 64
    assert t > 0, (n, pref)
    return t


def _ln(x):
    xc = x - jnp.mean(x, axis=-1, keepdims=True)
    return xc * lax.rsqrt(jnp.mean(xc * xc, axis=-1, keepdims=True) + LN_EPS)


def _dot(a, b):
    return jnp.dot(a, b, preferred_element_type=F32)


def _dot_t(a, b):
    return lax.dot_general(a, b, (((1,), (1,)), ((), ())), preferred_element_type=F32)


def _ada_kernel(c_ref, w_ref, b_ref, o_ref):
    c = c_ref[...]
    s = (c * jax.nn.sigmoid(c)).astype(BF16)
    o_ref[...] = _dot(s, w_ref[...].astype(BF16)) + b_ref[...]


def _ada(c, w_ada, b_ada):
    m, d = c.shape
    n = w_ada.shape[1]
    tn = _tile(n, 512)
    return pl.pallas_call(
        _ada_kernel,
        grid=(n // tn,),
        in_specs=[pl.BlockSpec((m, d), lambda j: (0, 0)),
                  pl.BlockSpec((d, tn), lambda j: (0, j)),
                  pl.BlockSpec((1, tn), lambda j: (0, j))],
        out_specs=pl.BlockSpec((m, tn), lambda j: (0, j)),
        out_shape=jax.ShapeDtypeStruct((m, n), F32),
        compiler_params=_cparams(("arbitrary",)),
        name="ada",
    )(c, w_ada, b_ada.reshape(1, n))


def _ln_mod_kernel(x_ref, shift_ref, scale_ref, h_ref):
    h_ref[...] = (_ln(x_ref[...]) * (1.0 + scale_ref[...]) + shift_ref[...]).astype(h_ref.dtype)


def _mod_spec(d, comp, tiles_per_seq, row0=0):
    return pl.BlockSpec((None, 1, d), lambda i: ((row0 + i // tiles_per_seq) * 6 + comp, 0, 0))


def _ln_mod(x, mod3, row0, seq_len, shift_comp, scale_comp):
    t, d = x.shape
    tm = _tile(seq_len, 256)
    tps = seq_len // tm
    return pl.pallas_call(
        _ln_mod_kernel,
        grid=(t // tm,),
        in_specs=[pl.BlockSpec((tm, d), lambda i: (i, 0)),
                  _mod_spec(d, shift_comp, tps, row0),
                  _mod_spec(d, scale_comp, tps, row0)],
        out_specs=pl.BlockSpec((tm, d), lambda i: (i, 0)),
        out_shape=jax.ShapeDtypeStruct((t, d), BF16),
        compiler_params=_cparams(("parallel",)),
        name="ln_mod",
    )(x, mod3, mod3)


def _proj_kernel(*refs, rope, scale, n_out):
    x_ref, w_ref = refs[0], refs[1]
    outs = refs[len(refs) - n_out:]
    acc = _dot(x_ref[...], w_ref[...])
    if rope:
        c_ref, s1_ref, s2_ref = refs[2:5]
        cos, s1, s2 = c_ref[...], s1_ref[...], s2_ref[...]
        parts = []
        for g in range(acc.shape[1] // LANES):
            blk = acc[:, g * LANES:(g + 1) * LANES]
            half = ROPE_DIMS // 2
            parts.append(blk * cos + pltpu.roll(blk, LANES - half, 1) * s1 + pltpu.roll(blk, half, 1) * s2)
        acc = jnp.concatenate(parts, axis=1) if len(parts) > 1 else parts[0]
    if scale != 1.0:
        acc = acc * scale
    for o in outs:
        o[...] = acc.astype(o.dtype)


def _proj(x, w, out_dtypes, rope_tabs=None, scale=1.0, tm_pref=512, tn_pref=1024):
    m, k = x.shape
    n = w.shape[1]
    tm, tn = _tile(m, tm_pref), _tile(n, tn_pref)
    in_specs = [pl.BlockSpec((tm, k), lambda i, j: (i, 0)),
                pl.BlockSpec((k, tn), lambda i, j: (0, j))]
    args = [x, w]
    if rope_tabs is not None:
        in_specs += [pl.BlockSpec((tm, LANES), lambda i, j: (i, 0))] * 3
        args += list(rope_tabs)
    outs = pl.pallas_call(
        functools.partial(_proj_kernel, rope=rope_tabs is not None, scale=scale, n_out=len(out_dtypes)),
        grid=(m // tm, n // tn),
        in_specs=in_specs,
        out_specs=[pl.BlockSpec((tm, tn), lambda i, j: (i, j)) for _ in out_dtypes],
        out_shape=[jax.ShapeDtypeStruct((m, n), dt) for dt in out_dtypes],
        compiler_params=_cparams(("parallel", "arbitrary")),
        name="proj",
    )(*args)
    return outs


def _glu_kernel(x_ref, wg_ref, wu_ref, o_ref):
    x = x_ref[...]
    g = _dot(x, wg_ref[...])
    u = _dot(x, wu_ref[...])
    o_ref[...] = (g * jax.nn.sigmoid(g) * u).astype(o_ref.dtype)


def _glu(x, wg, wu):
    m, k = x.shape
    n = wg.shape[1]
    tm, tn = _tile(m, 512), _tile(n, 512)
    return pl.pallas_call(
        _glu_kernel,
        grid=(m // tm, n // tn),
        in_specs=[pl.BlockSpec((tm, k), lambda i, j: (i, 0)),
                  pl.BlockSpec((k, tn), lambda i, j: (0, j)),
                  pl.BlockSpec((k, tn), lambda i, j: (0, j))],
        out_specs=pl.BlockSpec((tm, tn), lambda i, j: (i, j)),
        out_shape=jax.ShapeDtypeStruct((m, n), BF16),
        compiler_params=_cparams(("parallel", "arbitrary")),
        name="shared_glu",
    )(x, wg, wu)


def _merge_kernel(od_ref, os_ref, wd_ref, ws_ref, gd_ref, gs_ref, o_ref):
    bd = _dot(od_ref[...], wd_ref[...])
    bs = _dot(os_ref[...], ws_ref[...])
    gd = jax.nn.sigmoid(gd_ref[...].astype(F32))
    gs = jax.nn.sigmoid(gs_ref[...].astype(F32))
    o_ref[...] = (gd * bd + gs * bs).astype(o_ref.dtype)


def _merge(od, osb, wbd, wbs, gd, gs):
    m, k = od.shape
    n = wbd.shape[1]
    tm, tn = _tile(m, 512), _tile(n, 1024)
    return pl.pallas_call(
        _merge_kernel,
        grid=(m // tm, n // tn),
        in_specs=[pl.BlockSpec((tm, k), lambda i, j: (i, 0)),
                  pl.BlockSpec((tm, k), lambda i, j: (i, 0)),
                  pl.BlockSpec((k, tn), lambda i, j: (0, j)),
                  pl.BlockSpec((k, tn), lambda i, j: (0, j)),
                  pl.BlockSpec((tm, tn), lambda i, j: (i, j)),
                  pl.BlockSpec((tm, tn), lambda i, j: (i, j))],
        out_specs=pl.BlockSpec((tm, tn), lambda i, j: (i, j)),
        out_shape=jax.ShapeDtypeStruct((m, n), BF16),
        compiler_params=_cparams(("parallel", "arbitrary")),
        name="merge",
    )(od, osb, wbd, wbs, gd, gs)


def _lambda(lam_ref, lam_init):
    a = jnp.sum(lam_ref[0:1, :] * lam_ref[1:2, :], axis=-1, keepdims=True)
    b = jnp.sum(lam_ref[2:3, :] * lam_ref[3:4, :], axis=-1, keepdims=True)
    return jnp.exp(a) - jnp.exp(b) + lam_init


def _split_components(q):
    lane = lax.broadcasted_iota(jnp.int32, q.shape, 1)
    zero = jnp.zeros_like(q)
    return jnp.where(lane < DIFF_DK, q, zero), jnp.where(lane >= DIFF_DK, q, zero)


def _diff_finish(o1, l1, o2, l2, lam, g, lam_init):
    o = o1 / l1 - lam * (o2 / l2)
    o = o * lax.rsqrt(jnp.mean(o * o, axis=-1, keepdims=True) + LN_EPS)
    return o * g * (1.0 - lam_init)


def _diff_prompt_kernel(lam_ref, g_ref, q_ref, k_ref, v_ref, o_ref, m_ref, l_ref, acc_ref, *, tq, tk, lam_init):
    i = pl.program_id(1)
    q1, q2 = _split_components(q_ref[...])
    m_ref[...] = jnp.full(m_ref.shape, -jnp.inf, F32)
    l_ref[...] = jnp.zeros(l_ref.shape, F32)
    acc_ref[...] = jnp.zeros(acc_ref.shape, F32)

    def block(start, keep):
        k = k_ref[pl.ds(start, tk), :]
        v = v_ref[pl.ds(start, tk), :]
        for comp, qc in enumerate((q1, q2)):
            s = _dot_t(qc, k)
            if keep is not None:
                s = jnp.where(keep[0], jnp.where(keep[1], s, -1e30), -1e30)
            m_old = m_ref[comp]
            m_new = jnp.maximum(m_old, jnp.max(s, axis=-1, keepdims=True))
            alpha = jnp.exp(m_old - m_new)
            p = jnp.exp(s - m_new)
            l_ref[comp] = alpha * l_ref[comp] + jnp.sum(p, axis=-1, keepdims=True)
            acc_ref[comp] = alpha * acc_ref[comp] + _dot(p.astype(BF16), v)
            m_ref[comp] = m_new

    n_full = (i * tq) // tk

    def body(j, carry):
        block(pl.multiple_of(j * tk, tk), None)
        return carry

    lax.fori_loop(0, n_full, body, 0)
    start = pl.multiple_of(jnp.maximum((i + 1) * tq - tk, 0), tq)
    lo = n_full * tk
    col = start + lax.broadcasted_iota(jnp.int32, (tq, tk), 1)
    row = i * tq + lax.broadcasted_iota(jnp.int32, (tq, 1), 0)
    hi = (row // CHUNK + 1) * CHUNK
    block(start, (col >= lo, col < hi))
    lam = _lambda(lam_ref, lam_init)
    o = _diff_finish(acc_ref[0], l_ref[0], acc_ref[1], l_ref[1], lam, g_ref[...], lam_init)
    o_ref[...] = o.astype(o_ref.dtype)


def _diff_prompt(q, k, v, lam4, g, lam_init):
    t, w = q.shape
    heads = w // HEAD_W
    tq = _tile(t, 256)
    tk = _tile(t, 2048)
    assert tq % CHUNK == 0 and tk % tq == 0
    return pl.pallas_call(
        functools.partial(_diff_prompt_kernel, tq=tq, tk=tk, lam_init=lam_init),
        grid=(heads, t // tq),
        in_specs=[pl.BlockSpec((4, DIFF_DK), lambda h, i: (0, 0)),
                  pl.BlockSpec((1, HEAD_W), lambda h, i: (0, 0)),
                  pl.BlockSpec((tq, HEAD_W), lambda h, i: (i, h)),
                  pl.BlockSpec((t, HEAD_W), lambda h, i: (0, h)),
                  pl.BlockSpec((t, HEAD_W), lambda h, i: (0, h))],
        out_specs=pl.BlockSpec((tq, HEAD_W), lambda h, i: (i, h)),
        out_shape=jax.ShapeDtypeStruct((t, w), BF16),
        scratch_shapes=[pltpu.VMEM((2, tq, 1), F32), pltpu.VMEM((2, tq, 1), F32),
                        pltpu.VMEM((2, tq, HEAD_W), F32)],
        compiler_params=_cparams(("parallel", "arbitrary")),
        name="diff_prompt",
    )(lam4, g, q, k, v)


def _diff_sample_kernel(lam_ref, g_ref, q_ref, kn_ref, vn_ref, kct_ref, vc_ref, o_ref, *, past_len, lam_init):
    q1, q2 = _split_components(q_ref[...])
    kpt = kct_ref[...].astype(BF16)
    vp = vc_ref[...].astype(BF16)
    kn = kn_ref[...].astype(BF16)
    vn = vn_ref[...].astype(BF16)
    tn = kn.shape[0]
    r = (past_len + lax.broadcasted_iota(jnp.int32, (tn, tn), 0)) // CHUNK
    c = (past_len + lax.broadcasted_iota(jnp.int32, (tn, tn), 1)) // CHUNK
    keep = c <= r
    res = []
    for qc in (q1, q2):
        sp = _dot(qc, kpt)
        sn = jnp.where(keep, _dot_t(qc, kn), -1e30)
        m = jnp.maximum(jnp.max(sp, axis=-1, keepdims=True), jnp.max(sn, axis=-1, keepdims=True))
        pp = jnp.exp(sp - m)
        pn = jnp.exp(sn - m)
        l = jnp.sum(pp, axis=-1, keepdims=True) + jnp.sum(pn, axis=-1, keepdims=True)
        res.append((_dot(pp.astype(BF16), vp) + _dot(pn.astype(BF16), vn), l))
    lam = _lambda(lam_ref, lam_init)
    o = _diff_finish(res[0][0], res[0][1], res[1][0], res[1][1], lam, g_ref[...], lam_init)
    o_ref[...] = o.astype(o_ref.dtype)


def _diff_sample(q, kn, vn, kct, vc, lam4, g, lam_init, seq_len):
    t, w = q.shape
    b, heads, _, past_len = kct.shape
    assert past_len % CHUNK == 0
    row = pl.BlockSpec((seq_len, HEAD_W), lambda bi, h: (bi, h))
    cache = pl.BlockSpec((None, past_len, HEAD_W), lambda bi, h: (bi, 0, h))
    cache_t = pl.BlockSpec((None, None, HEAD_W, past_len), lambda bi, h: (bi, h, 0, 0))
    return pl.pallas_call(
        functools.partial(_diff_sample_kernel, past_len=past_len, lam_init=lam_init),
        grid=(b, heads),
        in_specs=[pl.BlockSpec((4, DIFF_DK), lambda bi, h: (0, 0)),
                  pl.BlockSpec((1, HEAD_W), lambda bi, h: (0, 0)),
                  row, row, row, cache_t, cache],
        out_specs=row,
        out_shape=jax.ShapeDtypeStruct((t, w), BF16),
        compiler_params=_cparams(("parallel", "arbitrary")),
        name="diff_sample",
    )(lam4, g, q, kn, vn, kct, vc)


def _head_rows(cache_ref, head, start, size, heads):
    return cache_ref[pl.ds(start * heads + head, size, stride=heads), :]


def _head_major(x, b, s):
    return x.reshape(b, s, -1, HEAD_W).transpose(0, 2, 1, 3)


def _token_major(x):
    b, h, s, w = x.shape
    return x.transpose(0, 2, 1, 3).reshape(b * s, h * w)


def _sample_specs(seq_len, heads, pc, chunk_index):
    row = pl.BlockSpec((None, heads, seq_len, HEAD_W), lambda bi, c: (bi, 0, 0, 0))
    cache = pl.BlockSpec((None, pc * heads, HEAD_W), lambda bi, c: (bi, chunk_index(c), 0))
    return row, cache


def _sb_block(q, k, v, carry_ref, acc_ref, keep):
    tk = k.shape[0]
    z = _dot_t(q, k)
    soft = jnp.log(1.0 + jnp.exp(-jnp.abs(z)))
    log_sig = jnp.minimum(z, 0.0) - soft
    log_keep = jnp.minimum(-z, 0.0) - soft
    if keep is not None:
        log_keep = jnp.where(keep, log_keep, 0.0)
    upper = (lax.broadcasted_iota(jnp.int32, (tk, tk), 0) > lax.broadcasted_iota(jnp.int32, (tk, tk), 1))
    upper = jnp.where(upper, 1.0, 0.0).astype(BF16)
    hi = log_keep.astype(BF16)
    lo = (log_keep - hi.astype(F32)).astype(BF16)
    later = _dot(hi, upper) + _dot(lo, upper) + carry_ref[...]
    w = jnp.exp(log_sig + later)
    if keep is not None:
        w = jnp.where(keep, w, 0.0)
    acc_ref[...] += _dot(w.astype(BF16), v)
    carry_ref[...] += jnp.sum(log_keep, axis=-1, keepdims=True)


def _sb_prompt_kernel(q_ref, k_ref, v_ref, o_ref, carry_ref, acc_ref, *, tq):
    i = pl.program_id(1)
    q = q_ref[...]
    carry_ref[...] = jnp.zeros(carry_ref.shape, F32)
    acc_ref[...] = jnp.zeros(acc_ref.shape, F32)

    def load(j):
        start = pl.multiple_of(j * tq, tq)
        return k_ref[pl.ds(start, tq), :], v_ref[pl.ds(start, tq), :]

    strict = (lax.broadcasted_iota(jnp.int32, (tq, tq), 1) < lax.broadcasted_iota(jnp.int32, (tq, tq), 0))
    k, v = load(i)
    _sb_block(q, k, v, carry_ref, acc_ref, strict)

    def cond(state):
        j, alive = state
        return jnp.logical_and(j >= 0, alive > SB_DEAD_LOG)

    def body(state):
        j, _ = state
        kj, vj = load(j)
        _sb_block(q, kj, vj, carry_ref, acc_ref, None)
        return j - 1, jnp.max(carry_ref[...])

    lax.while_loop(cond, body, (i - 1, jnp.max(carry_ref[...])))
    o_ref[...] = acc_ref[...].astype(o_ref.dtype)


def _sb_prompt(q, k, v):
    t, w = q.shape
    heads = w // HEAD_W
    tq = _tile(t, 256)
    return pl.pallas_call(
        functools.partial(_sb_prompt_kernel, tq=tq),
        grid=(heads, t // tq),
        in_specs=[pl.BlockSpec((tq, HEAD_W), lambda h, i: (i, h)),
                  pl.BlockSpec((t, HEAD_W), lambda h, i: (0, h)),
                  pl.BlockSpec((t, HEAD_W), lambda h, i: (0, h))],
        out_specs=pl.BlockSpec((tq, HEAD_W), lambda h, i: (i, h)),
        out_shape=jax.ShapeDtypeStruct((t, w), BF16),
        scratch_shapes=[pltpu.VMEM((tq, 1), F32), pltpu.VMEM((tq, HEAD_W), F32)],
        compiler_params=_cparams(("parallel", "arbitrary")),
        name="sb_prompt",
    )(q, k, v)


def _sb_sample_kernel(q_ref, kn_ref, vn_ref, kc_ref, vc_ref, o_ref, carry_ref, acc_ref, *, heads, pc, tk):
    c = pl.program_id(1)
    tn = q_ref.shape[1]

    @pl.when(c == 0)
    def _():
        carry_ref[...] = jnp.zeros(carry_ref.shape, F32)
        acc_ref[...] = jnp.zeros(acc_ref.shape, F32)
        strict = (lax.broadcasted_iota(jnp.int32, (tn, tn), 1) < lax.broadcasted_iota(jnp.int32, (tn, tn), 0))

        def new_rows(h, carry):
            _sb_block(q_ref[h], kn_ref[h].astype(BF16), vn_ref[h].astype(BF16), carry_ref.at[h], acc_ref.at[h], strict)
            return carry

        lax.fori_loop(0, heads, new_rows, 0)

    for j in reversed(range(pc // tk)):
        @pl.when(jnp.max(carry_ref[...]) > SB_DEAD_LOG)
        def _():
            def past_rows(h, carry):
                kj = _head_rows(kc_ref, h, j * tk, tk, heads).astype(BF16)
                vj = _head_rows(vc_ref, h, j * tk, tk, heads).astype(BF16)
                _sb_block(q_ref[h], kj, vj, carry_ref.at[h], acc_ref.at[h], None)
                return carry

            lax.fori_loop(0, heads, past_rows, 0)

    @pl.when(c == pl.num_programs(1) - 1)
    def _():
        o_ref[...] = acc_ref[...].astype(o_ref.dtype)


def _sb_sample(q, kn, vn, kc, vc, seq_len):
    t, w = q.shape
    heads = w // HEAD_W
    b = kc.shape[0]
    past_len = kc.shape[1] // heads
    pc = _tile(past_len, SAMPLE_PAST_CHUNK)
    n_c = past_len // pc
    tk = _tile(pc, 256)
    row, cache = _sample_specs(seq_len, heads, pc, lambda c: n_c - 1 - c)
    out = pl.pallas_call(
        functools.partial(_sb_sample_kernel, heads=heads, pc=pc, tk=tk),
        grid=(b, n_c),
        in_specs=[row, row, row, cache, cache],
        out_specs=row,
        out_shape=jax.ShapeDtypeStruct((b, heads, seq_len, HEAD_W), BF16),
        scratch_shapes=[pltpu.VMEM((heads, seq_len, 1), F32), pltpu.VMEM((heads, seq_len, HEAD_W), F32)],
        compiler_params=_cparams(("parallel", "arbitrary")),
        name="sb_sample",
    )(_head_major(q, b, seq_len), _head_major(kn, b, seq_len), _head_major(vn, b, seq_len), kc, vc)
    return _token_major(out)


def _pack_halves(x):
    n = x.shape[1] // 2
    bits = pltpu.bitcast(x.astype(BF16).astype(F32), jnp.uint32)
    return bits[:, n:] | (bits[:, :n] >> 16)


def _unpack_halves(u):
    lo = pltpu.bitcast(u << 16, F32)
    hi = pltpu.bitcast(u & jnp.uint32(0xFFFF0000), F32)
    return lo, hi


def _resid_norm_kernel(x_ref, a_ref, gate_ref, g_ref, b_ref, shift_ref, scale_ref, x1_ref, h_ref, hf_ref, *, alpha):
    x1 = _ln(alpha * x_ref[...] + gate_ref[...] * a_ref[...]) * g_ref[...] + b_ref[...]
    x1_ref[...] = x1
    h = _ln(x1) * (1.0 + scale_ref[...]) + shift_ref[...]
    h_ref[...] = h.astype(h_ref.dtype)
    hf_ref[...] = h


def _resid_norm(x, a, mod3, row0, seq_len, g, b, alpha):
    t, d = x.shape
    tm = _tile(seq_len, 256)
    tps = seq_len // tm
    tok = pl.BlockSpec((tm, d), lambda i: (i, 0))
    vec = pl.BlockSpec((1, d), lambda i: (0, 0))
    return pl.pallas_call(
        functools.partial(_resid_norm_kernel, alpha=alpha),
        grid=(t // tm,),
        in_specs=[tok, tok, _mod_spec(d, 2, tps, row0), vec, vec,
                  _mod_spec(d, 3, tps, row0), _mod_spec(d, 4, tps, row0)],
        out_specs=[tok, tok, tok],
        out_shape=[jax.ShapeDtypeStruct((t, d), F32), jax.ShapeDtypeStruct((t, d), BF16),
                   jax.ShapeDtypeStruct((t, d), F32)],
        compiler_params=_cparams(("parallel",)),
        name="resid_norm",
    )(x, a, mod3, g, b, mod3, mod3)


def _router_kernel(h_ref, whi_ref, wlo_ref, bias_ref, gate_ref, idx_ref, rank_ref, cnt_ref, run_ref):
    @pl.when(pl.program_id(0) == 0)
    def _():
        run_ref[...] = jnp.zeros(run_ref.shape, F32)

    h = h_ref[...]
    logits = _dot(h, whi_ref[...]) + _dot(h, wlo_ref[...])
    scores = jax.nn.sigmoid(logits)
    tm, e = scores.shape
    lane = lax.broadcasted_iota(jnp.int32, (tm, e), 1)
    sel = scores + bias_ref[...]
    picked = jnp.zeros((tm, e), F32)
    idx = jnp.zeros((tm, e), jnp.int32)
    firsts = []
    for k in range(TOP_K):
        best = jnp.max(sel, axis=-1, keepdims=True)
        first = jnp.min(jnp.where(sel == best, lane, e), axis=-1, keepdims=True)
        hit = lane == first
        picked = jnp.where(hit, 1.0, picked)
        sel = jnp.where(hit, -jnp.inf, sel)
        idx = jnp.where(lane == k, first, idx)
        firsts.append(first)
    gate = scores * picked
    gate = gate / jnp.sum(gate, axis=-1, keepdims=True) * ROUTE_SCALE
    lower = (lax.broadcasted_iota(jnp.int32, (tm, tm), 1) < lax.broadcasted_iota(jnp.int32, (tm, tm), 0))
    within = _dot(jnp.where(lower, 1.0, 0.0).astype(BF16), picked.astype(BF16)) + run_ref[...]
    gate8 = jnp.zeros((tm, e), F32)
    rank8 = jnp.zeros((tm, e), F32)
    for k, first in enumerate(firsts):
        hit = lane == first
        gk = jnp.sum(jnp.where(hit, gate, 0.0), axis=-1, keepdims=True)
        rk = jnp.sum(jnp.where(hit, within, 0.0), axis=-1, keepdims=True)
        gate8 = jnp.where(lane == k, gk, gate8)
        rank8 = jnp.where(lane == k, rk, rank8)
    gate_ref[...] = gate8
    idx_ref[...] = idx
    rank_ref[...] = rank8.astype(jnp.int32)
    run_ref[...] += jnp.sum(picked, axis=0, keepdims=True)
    cnt_ref[...] = run_ref[...].astype(jnp.int32)


def _router(h, w_router, router_bias):
    t, d = h.shape
    e = w_router.shape[1]
    assert e == LANES
    tm = _tile(t, 512)
    whi = w_router.astype(BF16)
    wlo = (w_router - whi.astype(F32)).astype(BF16)
    tok = pl.BlockSpec((tm, e), lambda i: (i, 0))
    wspec = pl.BlockSpec((d, e), lambda i: (0, 0))
    one = pl.BlockSpec((1, e), lambda i: (0, 0))
    return pl.pallas_call(
        _router_kernel,
        grid=(t // tm,),
        in_specs=[pl.BlockSpec((tm, d), lambda i: (i, 0)), wspec, wspec, one],
        out_specs=[tok, tok, tok, one],
        out_shape=[jax.ShapeDtypeStruct((t, e), F32), jax.ShapeDtypeStruct((t, e), jnp.int32),
                   jax.ShapeDtypeStruct((t, e), jnp.int32), jax.ShapeDtypeStruct((1, e), jnp.int32)],
        scratch_shapes=[pltpu.VMEM((1, e), F32)],
        compiler_params=_cparams(("arbitrary",)),
        name="router",
    )(h, whi, wlo, router_bias.reshape(1, e))


def _row_copy(src_hbm, tok, buf, slot, r, sem):
    return pltpu.make_async_copy(src_hbm.at[pl.ds(tok, 1)], buf.at[slot, pl.ds(r, 1)], sem.at[slot])


def _expert_weights(plan_ref, b, w_hbms, w_bufs, w_sem):
    expert, first, slot = plan_ref[0, b], plan_ref[1, b], plan_ref[2, b]
    has_next, nxt = plan_ref[3, b], plan_ref[4, b]

    def copies(e, into):
        return [pltpu.make_async_copy(w.at[e], buf.at[into], w_sem.at[into, n])
                for n, (w, buf) in enumerate(zip(w_hbms, w_bufs))]

    @pl.when(b == 0)
    def _():
        for cp in copies(expert, slot):
            cp.start(priority=1)

    @pl.when(first == 1)
    def _():
        for cp in copies(expert, slot):
            cp.wait()

    @pl.when(jnp.logical_and(first == 1, has_next == 1))
    def _():
        for cp in copies(nxt, 1 - slot):
            cp.start(priority=1)

    return slot


def _expert_up_kernel(plan_ref, nused_ref, tok_ref, tok_next_ref, h_hbm, wg_hbm, wu_hbm, o_ref, buf, sem,
                      wg_buf, wu_buf, w_sem, *, blk, n_blocks):
    b = pl.program_id(0)
    n_used = nused_ref[0]
    slot = lax.rem(b, 2)

    def start_rows(toks, into):
        for r in range(blk):
            _row_copy(h_hbm, toks[0, r], buf, into, r, sem).start()

    def wait_rows(into):
        for r in range(blk):
            _row_copy(h_hbm, 0, buf, into, r, sem).wait()

    @pl.when(b == 0)
    def _():
        def body(r, carry):
            _row_copy(h_hbm, tok_ref[0, r], buf, slot, r, sem).start()
            return carry
        lax.fori_loop(0, blk, body, 0)

    @pl.when(b < n_used)
    def _():
        ws = _expert_weights(plan_ref, b, (wg_hbm, wu_hbm), (wg_buf, wu_buf), w_sem)
        wait_rows(slot)
        kc = buf.shape[2] // GATHER_BATCHES
        rows = blk // GATHER_BATCHES
        g = u = None
        for c in range(GATHER_BATCHES):
            for r in range(c * rows, (c + 1) * rows):
                _row_copy(h_hbm, tok_next_ref[0, r], buf, 1 - slot, r, sem).start()
            xc = buf[slot, :, c * kc:(c + 1) * kc].astype(BF16)
            gc = _dot(xc, wg_buf[ws, c * kc:(c + 1) * kc, :].astype(BF16))
            uc = _dot(xc, wu_buf[ws, c * kc:(c + 1) * kc, :].astype(BF16))
            g = gc if g is None else g + gc
            u = uc if u is None else u + uc
        o_ref[...] = (g * jax.nn.sigmoid(g) * u).astype(o_ref.dtype)

        @pl.when(b == n_blocks - 1)
        def _():
            wait_rows(1 - slot)

    @pl.when(b == n_used)
    def _():
        wait_rows(slot)

    @pl.when(b >= n_used)
    def _():
        o_ref[...] = jnp.zeros(o_ref.shape, o_ref.dtype)


def _expert_up(h, wg, wu, slot_tok, plan, n_used, blk):
    e, d, hid = wg.shape
    n_blocks = plan.shape[1]
    tok3 = slot_tok.reshape(n_blocks, 1, blk)
    grid_spec = pltpu.PrefetchScalarGridSpec(
        num_scalar_prefetch=2,
        grid=(n_blocks,),
        in_specs=[pl.BlockSpec((None, 1, blk), lambda b, pr, nu: (b, 0, 0), memory_space=pltpu.SMEM),
                  pl.BlockSpec((None, 1, blk), lambda b, pr, nu: (jnp.minimum(b + 1, n_blocks - 1), 0, 0),
                               memory_space=pltpu.SMEM),
                  pl.BlockSpec(memory_space=pl.ANY), pl.BlockSpec(memory_space=pl.ANY),
                  pl.BlockSpec(memory_space=pl.ANY)],
        out_specs=pl.BlockSpec((blk, hid), lambda b, pr, nu: (b, 0)),
        scratch_shapes=[pltpu.VMEM((2, blk, d), F32), pltpu.SemaphoreType.DMA((2,)),
                        pltpu.VMEM((2, d, hid), F32), pltpu.VMEM((2, d, hid), F32),
                        pltpu.SemaphoreType.DMA((2, 2))],
    )
    return pl.pallas_call(
        functools.partial(_expert_up_kernel, blk=blk, n_blocks=n_blocks),
        grid_spec=grid_spec,
        out_shape=jax.ShapeDtypeStruct((n_blocks * blk, hid), BF16),
        compiler_params=_cparams(("arbitrary",)),
        name="expert_up",
    )(plan, n_used, tok3, tok3, h, wg, wu)


def _expert_down_kernel(plan_ref, nused_ref, a_ref, wd_hbm, o_ref, wd_buf, w_sem):
    b = pl.program_id(0)

    @pl.when(b < nused_ref[0])
    def _():
        ws = _expert_weights(plan_ref, b, (wd_hbm,), (wd_buf,), w_sem)
        o_ref[...] = _pack_halves(_dot(a_ref[...], wd_buf[ws].astype(BF16)))

    @pl.when(b >= nused_ref[0])
    def _():
        o_ref[...] = jnp.zeros(o_ref.shape, o_ref.dtype)


def _expert_down(act, wd, plan, n_used, blk):
    e, hid, d = wd.shape
    n_blocks = plan.shape[1]
    grid_spec = pltpu.PrefetchScalarGridSpec(
        num_scalar_prefetch=2,
        grid=(n_blocks,),
        in_specs=[pl.BlockSpec((blk, hid), lambda b, pr, nu: (b, 0)),
                  pl.BlockSpec(memory_space=pl.ANY)],
        out_specs=pl.BlockSpec((blk, d // 2), lambda b, pr, nu: (b, 0)),
        scratch_shapes=[pltpu.VMEM((2, hid, d), F32), pltpu.SemaphoreType.DMA((2, 1))],
    )
    return pl.pallas_call(
        _expert_down_kernel,
        grid_spec=grid_spec,
        out_shape=jax.ShapeDtypeStruct((n_blocks * blk, d // 2), jnp.uint32),
        compiler_params=_cparams(("arbitrary",)),
        name="expert_down",
    )(plan, n_used, act, wd)


def _combine_kernel(dest_ref, dest_next_ref, y_hbm, gate_ref, sh_ref, x_ref, gf_ref, g_ref, b_ref, o_ref, buf, sem, *,
                    tt, n_steps, alpha):
    i = pl.program_id(0)
    slot = lax.rem(i, 2)

    def copy(src_row, into, r, k):
        return pltpu.make_async_copy(y_hbm.at[pl.ds(src_row, 1)], buf.at[into, k, pl.ds(r, 1)], sem.at[into])

    def start_rows(dref, into):
        for r in range(tt):
            for k in range(TOP_K):
                copy(dref[0, r * TOP_K + k], into, r, k).start()

    def wait_rows(into):
        for r in range(tt):
            for k in range(TOP_K):
                copy(0, into, r, k).wait()

    @pl.when(i == 0)
    def _():
        def body(r, carry):
            for k in range(TOP_K):
                copy(dest_ref[0, r * TOP_K + k], slot, r, k).start()
            return carry
        lax.fori_loop(0, tt, body, 0)

    wait_rows(slot)
    start_rows(dest_next_ref, 1 - slot)
    gate = gate_ref[...]
    n = buf.shape[-1]
    f_lo, f_hi = sh_ref[:, :n], sh_ref[:, n:]
    for k in range(TOP_K):
        y_lo, y_hi = _unpack_halves(buf[slot, k])
        f_lo = f_lo + y_lo * gate[:, k:k + 1]
        f_hi = f_hi + y_hi * gate[:, k:k + 1]
    f = jnp.concatenate([f_lo, f_hi], axis=1)
    o_ref[...] = _ln(alpha * x_ref[...] + gf_ref[...] * f) * g_ref[...] + b_ref[...]

    @pl.when(i == n_steps - 1)
    def _():
        wait_rows(1 - slot)


def _combine(y_slots, dest8, gate8, shared, x1, mod3, row0, seq_len, tok0, n_tok, g, b, alpha):
    d = x1.shape[1]
    tt = _tile(seq_len, 64)
    assert tok0 % tt == 0
    tile0 = tok0 // tt
    tps = seq_len // tt
    n_steps = n_tok // tt
    tok = pl.BlockSpec((tt, d), lambda i: (tile0 + i, 0))
    vec = pl.BlockSpec((1, d), lambda i: (0, 0))
    dest3 = dest8.reshape(-1, 1, tt * TOP_K)
    return pl.pallas_call(
        functools.partial(_combine_kernel, tt=tt, n_steps=n_steps, alpha=alpha),
        grid=(n_steps,),
        in_specs=[pl.BlockSpec((None, 1, tt * TOP_K), lambda i: (tile0 + i, 0, 0), memory_space=pltpu.SMEM),
                  pl.BlockSpec((None, 1, tt * TOP_K), lambda i: (tile0 + jnp.minimum(i + 1, n_steps - 1), 0, 0),
                               memory_space=pltpu.SMEM),
                  pl.BlockSpec(memory_space=pl.ANY),
                  pl.BlockSpec((tt, LANES), lambda i: (tile0 + i, 0)),
                  tok, tok,
                  pl.BlockSpec((None, 1, d), lambda i: ((row0 + i // tps) * 6 + 5, 0, 0)),
                  vec, vec],
        out_specs=pl.BlockSpec((tt, d), lambda i: (i, 0)),
        out_shape=jax.ShapeDtypeStruct((n_tok, d), F32),
        scratch_shapes=[pltpu.VMEM((2, TOP_K, tt, d // 2), jnp.uint32), pltpu.SemaphoreType.DMA((2,))],
        compiler_params=_cparams(("arbitrary",)),
        name="combine",
    )(dest3, dest3, y_slots, gate8, shared, x1, mod3, g, b)


def _rope_tables(pos):
    half = ROPE_DIMS // 2
    inv_freq = ROPE_THETA ** (-jnp.arange(half, dtype=F32) / half)
    ang = pos.astype(F32)[:, None] * inv_freq[None, :]
    cos, sin = jnp.cos(ang), jnp.sin(ang)
    n = pos.shape[0]
    ones = jnp.ones((n, DIFF_DK - ROPE_DIMS), F32)
    zeros = jnp.zeros((n, DIFF_DK - ROPE_DIMS), F32)
    zh = jnp.zeros((n, half), F32)
    c64 = jnp.concatenate([cos, cos, ones], axis=1)
    s1 = jnp.concatenate([-sin, zh, zeros], axis=1)
    s2 = jnp.concatenate([zh, sin, zeros], axis=1)
    rep = LANES // DIFF_DK
    return tuple(jnp.tile(a, (1, rep)) for a in (c64, s1, s2))


def _mixer(x, mod3, row0, seq_len, past, wts, lam4, g_norm, lam_init):
    (w_qd, w_kd, w_vd, w_qs, w_ks, w_vs, w_gd, w_gs, w_bd, w_bs, w_out) = wts
    t = x.shape[0]
    past_len = 0 if past is None else past[0].shape[-1]
    pos = past_len + jnp.arange(seq_len, dtype=jnp.int32)
    tabs = _rope_tables(jnp.tile(pos, t // seq_len))
    h = _ln_mod(x, mod3, row0, seq_len, 0, 1)
    (qd,) = _proj(h, w_qd, (BF16,), rope_tabs=tabs, scale=DIFF_DK ** -0.5)
    (qs,) = _proj(h, w_qs, (BF16,), scale=HEAD_W ** -0.5)
    (gd,) = _proj(h, w_gd, (BF16,))
    (gs,) = _proj(h, w_gs, (BF16,))
    if past is None:
        kd, kd_b = _proj(h, w_kd, (F32, BF16), rope_tabs=tabs)
        vd, vd_b = _proj(h, w_vd, (F32, BF16))
        ks, ks_b = _proj(h, w_ks, (F32, BF16))
        vs, vs_b = _proj(h, w_vs, (F32, BF16))
        od = _diff_prompt(qd, kd_b, vd_b, lam4, g_norm, lam_init)
        osb = _sb_prompt(qs, ks_b, vs_b)
    else:
        (kd,) = _proj(h, w_kd, (F32,), rope_tabs=tabs)
        (vd,) = _proj(h, w_vd, (F32,))
        (ks,) = _proj(h, w_ks, (F32,))
        (vs,) = _proj(h, w_vs, (F32,))
        od = _diff_sample(qd, kd, vd, past[0], past[1], lam4, g_norm, lam_init, seq_len)
        osb = _sb_sample(qs, ks, vs, past[2], past[3], seq_len)
    merged = _merge(od, osb, w_bd, w_bs, gd, gs)
    (a,) = _proj(merged, w_out, (F32,))
    return a, (kd, vd, ks, vs)


def _moe_plan(idx8, rank8, counts, n_tok, blk):
    e = counts.shape[0]
    padded = (counts + blk - 1) // blk * blk
    pad_ends = jnp.cumsum(padded)
    pad_starts = pad_ends - padded
    hit = idx8[..., None] == jnp.arange(e, dtype=jnp.int32)
    dest8 = (jnp.sum(jnp.where(hit, pad_starts, 0), axis=-1) + rank8).astype(jnp.int32)
    n_blocks = -(-n_tok * TOP_K // blk) + e
    tok_ids = jnp.repeat(jnp.arange(n_tok, dtype=jnp.int32), TOP_K)
    slot_tok = jnp.zeros((n_blocks * blk,), jnp.int32).at[dest8.reshape(-1)].set(tok_ids)
    block_pos = jnp.arange(n_blocks, dtype=jnp.int32) * blk
    block_expert = jnp.minimum(jnp.searchsorted(pad_ends, block_pos, side='right'), e - 1).astype(jnp.int32)
    n_used = (pad_ends[-1] // blk).astype(jnp.int32).reshape(1)
    bidx = jnp.arange(n_blocks, dtype=jnp.int32)
    first = jnp.concatenate([jnp.ones((1,), bool), block_expert[1:] != block_expert[:-1]]) & (bidx < n_used[0])
    wslot = (jnp.cumsum(first.astype(jnp.int32)) - 1) % 2
    first_pos = jnp.where(first, bidx, n_blocks)
    later_first = jnp.concatenate([lax.cummin(first_pos, reverse=True)[1:], jnp.full((1,), n_blocks, jnp.int32)])
    has_next = later_first < n_blocks
    next_expert = block_expert[jnp.minimum(later_first, n_blocks - 1)]
    plan = jnp.stack([block_expert, first.astype(jnp.int32), wslot.astype(jnp.int32), has_next.astype(jnp.int32),
                      next_expert]).astype(jnp.int32)
    return dest8, slot_tok, plan, n_used


def kernel(x_prompt, x_sample, c_prompt, c_sample, cache_diff_k, cache_diff_v, cache_sb_k, cache_sb_v, w_ada, b_ada, w_in, w_br_diff, w_br_sb, w_out, lam_q1, lam_k1, lam_q2, lam_k2, diff_norm_g, ln1_g, ln1_b, w_router, router_bias, w_e_gate, w_e_up, w_e_down, w_sh_gate, w_sh_up, w_sh_down, ln2_g, ln2_b):
    depth = w_ada.shape[0]
    assert depth == 1
    bp, sp, d = x_prompt.shape
    bs, ss, _ = x_sample.shape
    assert bp == 1
    diff_w = w_br_diff.shape[1]
    sb_w = w_br_sb.shape[1]
    heads_d, heads_s = diff_w // HEAD_W, sb_w // HEAD_W
    past_len = cache_diff_k.shape[2]
    alpha = (2.0 * depth) ** 0.25
    lam_init = 0.8 - 0.6 * float(np.exp(-0.3 * 0))
    blk = 256

    widths = [diff_w, diff_w, diff_w, sb_w, sb_w, sb_w, d, d]
    offs = np.concatenate([[0], np.cumsum(widths)])
    w_in0 = w_in[0]
    w_qd, w_kd, w_vd, w_qs, w_ks, w_vs, w_gd, w_gs = (
        w_in0[:, offs[i]:offs[i + 1]].astype(BF16) for i in range(8))
    wts = (w_qd, w_kd, w_vd, w_qs, w_ks, w_vs, w_gd, w_gs,
           w_br_diff[0].astype(BF16), w_br_sb[0].astype(BF16), w_out[0].astype(BF16))
    lam4 = jnp.stack([lam_q1[0], lam_k1[0], lam_q2[0], lam_k2[0]]).astype(F32)
    g_norm = diff_norm_g[0].reshape(1, HEAD_W).astype(F32)

    c_all = jnp.concatenate([c_prompt, c_sample], axis=0)
    n_seq = c_all.shape[0]
    c_pad = jnp.pad(c_all, ((0, -n_seq % 8), (0, 0)))
    mod = _ada(c_pad, w_ada[0], b_ada[0])
    mod3 = mod.reshape(c_pad.shape[0] * 6, 1, d)

    xp = x_prompt.reshape(bp * sp, d)
    xs = x_sample.reshape(bs * ss, d)
    past = (jnp.transpose(cache_diff_k[0], (0, 2, 3, 4, 1)).reshape(bs, heads_d, HEAD_W, past_len),
            cache_diff_v[0].reshape(bs, past_len, diff_w),
            cache_sb_k[0].reshape(bs, past_len * heads_s, HEAD_W), cache_sb_v[0].reshape(bs, past_len * heads_s, HEAD_W))

    a_p, rows_p = _mixer(xp, mod3, 0, sp, None, wts, lam4, g_norm, lam_init)
    a_s, rows_s = _mixer(xs, mod3, bp, ss, past, wts, lam4, g_norm, lam_init)
    g1, b1 = ln1_g[0].reshape(1, d), ln1_b[0].reshape(1, d)
    x1_p, h2_p, h2f_p = _resid_norm(xp, a_p, mod3, 0, sp, g1, b1, alpha)
    x1_s, h2_s, h2f_s = _resid_norm(xs, a_s, mod3, bp, ss, g1, b1, alpha)

    n_p, n_s = bp * sp, bs * ss
    n_tok = n_p + n_s
    h2 = jnp.concatenate([h2_p, h2_s], axis=0)
    h2f = jnp.concatenate([h2f_p, h2f_s], axis=0)
    x1 = jnp.concatenate([x1_p, x1_s], axis=0)
    gate8, idx8, rank8, counts = _router(h2, w_router[0], router_bias[0])
    dest8, slot_tok, plan, n_used = _moe_plan(idx8[:, :TOP_K], rank8[:, :TOP_K], counts[0], n_tok, blk)
    act = _expert_up(h2f, w_e_gate[0], w_e_up[0], slot_tok, plan, n_used, blk)
    y_slots = _expert_down(act, w_e_down[0], plan, n_used, blk)
    sh_act = _glu(h2, w_sh_gate[0].astype(BF16), w_sh_up[0].astype(BF16))
    (shared,) = _proj(sh_act, w_sh_down[0].astype(BF16), (F32,))
    g2, b2 = ln2_g[0].reshape(1, d), ln2_b[0].reshape(1, d)
    y_p = _combine(y_slots, dest8, gate8, shared, x1, mod3, 0, sp, 0, n_p, g2, b2, alpha)
    y_s = _combine(y_slots, dest8, gate8, shared, x1, mod3, bp, ss, n_p, n_s, g2, b2, alpha)

    def rows(r, b, s):
        kd, vd, ks, vs = r
        return (kd.reshape(1, b, s, heads_d, 2, DIFF_DK), vd.reshape(1, b, s, heads_d, HEAD_W),
                ks.reshape(1, b, s, heads_s, HEAD_W), vs.reshape(1, b, s, heads_s, HEAD_W))

    return (y_p.reshape(bp, sp, d), y_s.reshape(bs, ss, d)) + rows(rows_p, bp, sp) + rows(rows_s, bs, ss)
```
